```python
import jax, jax.numpy as jnp
from jax import lax
import numpy as np

D_MODEL = 4096
BATCH = 16
SEQ = 256
DEPTH = 2
DEC_BATCH = 4
DEC_SEQ = 2048
PAST_LEN = 256

GRID_W = 64
N_MIXERS = 2
N_ATTN_LAYERS = (DEPTH + N_MIXERS - 1) // N_MIXERS
N_CHUNK_LAYERS = DEPTH // N_MIXERS
N_HEADS = 32
N_KV_HEADS = 8
HEAD_DIM = D_MODEL // N_HEADS
Q_DIM = N_HEADS * HEAD_DIM
KV_DIM = N_KV_HEADS * HEAD_DIM
AXIS_DIM = HEAD_DIM // 2
ROPE_THETA = 10000.0
Q_BLOCK = 128
CHUNK = 128
D_INNER = D_MODEL
N_SGU_GROUPS = 8
SGU_GROUP_DIM = D_INNER // N_SGU_GROUPS
N_EXPERT_GROUPS = 4
EXPERTS_PER_GROUP = 8
N_EXPERTS = N_EXPERT_GROUPS * EXPERTS_PER_GROUP
TOP_K = 2
D_EXPERT = D_MODEL // 4
N_MOD = 6
RMS_EPS = 1e-6

kernel_name = "hybrid_dit_gqa_chunkmlp_hmoe_step"


def rms_norm(x, gain):
    xf = x.astype(jnp.float32)
    y = xf * lax.rsqrt(jnp.mean(xf * xf, axis=-1, keepdims=True) + RMS_EPS)
    return (y * gain.astype(jnp.float32)).astype(x.dtype)


def modulation(cond, w_mod, b_mod):
    m = jax.nn.silu(cond) @ w_mod + b_mod
    return m.reshape(cond.shape[0], N_MOD, 1, D_MODEL)


def adaln(x, gain, shift, scale):
    return rms_norm(x, gain) * (1 + scale) + shift


def axial_rope(x):
    T = x.shape[1]
    rows = T // GRID_W
    row_idx = jnp.repeat(jnp.arange(rows), GRID_W).astype(jnp.float32)
    col_idx = jnp.tile(jnp.arange(GRID_W), rows).astype(jnp.float32)
    inv_freq = ROPE_THETA ** (-jnp.arange(0, AXIS_DIM, 2, dtype=jnp.float32) / AXIS_DIM)

    def rot(xa, pos):
        ang = pos[:, None] * inv_freq[None, :]
        cos = jnp.cos(ang)[None, :, None, :]
        sin = jnp.sin(ang)[None, :, None, :]
        x1, x2 = jnp.split(xa, 2, axis=-1)
        return jnp.concatenate([x1 * cos - x2 * sin, x2 * cos + x1 * sin], axis=-1)

    xf = x.astype(jnp.float32)
    out = jnp.concatenate([rot(xf[..., :AXIS_DIM], row_idx), rot(xf[..., AXIS_DIM:], col_idx)], axis=-1)
    return out.astype(x.dtype)


def qkv_proj(h, w_qkv, q_gain, k_gain):
    B, T, _ = h.shape
    qkv = h @ w_qkv
    q = qkv[..., :Q_DIM].reshape(B, T, N_HEADS, HEAD_DIM)
    k = qkv[..., Q_DIM:Q_DIM + KV_DIM].reshape(B, T, N_KV_HEADS, HEAD_DIM)
    v = qkv[..., Q_DIM + KV_DIM:].reshape(B, T, N_KV_HEADS, HEAD_DIM)
    return rms_norm(q, q_gain), rms_norm(k, k_gain), v


def block_attention(q, k, v):
    B, T, H, Dh = q.shape
    KV = k.shape[2]
    G = H // KV
    nb = T // Q_BLOCK
    qb = q.reshape(B, nb, Q_BLOCK, KV, G, Dh).transpose(1, 0, 2, 3, 4, 5)
    scale = HEAD_DIM ** -0.5

    def one(qblk):
        s = jnp.einsum('bqkgd,bskd->bkgqs', qblk, k, preferred_element_type=jnp.float32) * scale
        p = jax.nn.softmax(s, axis=-1).astype(v.dtype)
        return jnp.einsum('bkgqs,bskd->bqkgd', p, v)

    o = lax.map(one, qb)
    return o.transpose(1, 0, 2, 3, 4, 5).reshape(B, T, H * Dh)


def chunk_mlp(h, w_in, v_gain, w_s, b_s, w_out):
    B, T, _ = h.shape
    uv = jax.nn.gelu(h @ w_in)
    u, v = jnp.split(uv, 2, axis=-1)
    v = rms_norm(v, v_gain).reshape(B, T // CHUNK, CHUNK, N_SGU_GROUPS, SGU_GROUP_DIM)
    vm = jnp.einsum('gij,bnjgc->bnigc', w_s, v) + b_s.T[None, None, :, :, None]
    return (u * vm.reshape(B, T, D_INNER)) @ w_out


def hier_moe(h, w_group, b_group, w_expert, b_expert, w_gate_up, w_down):
    shape = h.shape
    x = h.reshape(-1, D_MODEL)
    T = x.shape[0]
    pg = jax.nn.softmax((x @ w_group).astype(jnp.float32) + b_group.astype(jnp.float32), axis=-1)
    g = jnp.argmax(pg, axis=-1)
    pg_top = jnp.take_along_axis(pg, g[:, None], axis=1)[:, 0]
    le = ((x @ w_expert).astype(jnp.float32) + b_expert.astype(jnp.float32)).reshape(T, N_EXPERT_GROUPS, EXPERTS_PER_GROUP)
    le_g = jnp.take_along_axis(le, g[:, None, None], axis=1)[:, 0]
    pe = jax.nn.softmax(le_g, axis=-1)
    top_p, top_i = lax.top_k(pe, TOP_K)
    top_p = top_p / jnp.sum(top_p, axis=-1, keepdims=True)
    gates = (pg_top[:, None] * top_p).reshape(-1)
    eid = (g[:, None] * EXPERTS_PER_GROUP + top_i).reshape(-1).astype(jnp.int32)
    order = jnp.argsort(eid)
    tok = order // TOP_K
    xs = x[tok]
    group_sizes = jnp.bincount(eid, length=N_EXPERTS).astype(jnp.int32)
    gu = lax.ragged_dot(xs, w_gate_up, group_sizes)
    a, b = jnp.split(gu, 2, axis=-1)
    ys = lax.ragged_dot(jax.nn.silu(a) * b, w_down, group_sizes)
    ys = ys * gates[order][:, None].astype(ys.dtype)
    y = jax.ops.segment_sum(ys, tok, num_segments=T)
    return y.reshape(shape)


def setup_inputs(seed: int = 0) -> dict:
    key = jax.random.key(seed)
    ks = jax.random.split(key, 32)
    f32 = jnp.float32

    def nrm(k, shape, scale):
        return jax.random.normal(k, shape, f32) * scale

    return {
        "x_prompt": nrm(ks[0], (BATCH, SEQ, D_MODEL), 1.0),
        "x_sample": nrm(ks[1], (DEC_BATCH, DEC_SEQ, D_MODEL), 1.0),
        "cache_k": nrm(ks[2], (DEC_BATCH, N_ATTN_LAYERS, PAST_LEN, N_KV_HEADS, HEAD_DIM), 1.0),
        "cache_v": nrm(ks[3], (DEC_BATCH, N_ATTN_LAYERS, PAST_LEN, N_KV_HEADS, HEAD_DIM), 1.0),
        "c": nrm(ks[4], (DEC_BATCH, D_MODEL), 1.0),
        "c_ctx": nrm(ks[5], (D_MODEL,), 1.0),
        "norm_gain": 1.0 + nrm(ks[6], (DEPTH, 2, D_MODEL), 0.1),
        "w_mod": nrm(ks[7], (DEPTH, D_MODEL, N_MOD * D_MODEL), 0.5 * D_MODEL ** -0.5),
        "b_mod": nrm(ks[8], (DEPTH, N_MOD * D_MODEL), 0.02),
        "attn_w_qkv": nrm(ks[9], (N_ATTN_LAYERS, D_MODEL, Q_DIM + 2 * KV_DIM), D_MODEL ** -0.5),
        "attn_q_gain": 1.0 + nrm(ks[10], (N_ATTN_LAYERS, HEAD_DIM), 0.1),
        "attn_k_gain": 1.0 + nrm(ks[11], (N_ATTN_LAYERS, HEAD_DIM), 0.1),
        "attn_w_o": nrm(ks[12], (N_ATTN_LAYERS, Q_DIM, D_MODEL), Q_DIM ** -0.5),
        "cm_w_in": nrm(ks[13], (N_CHUNK_LAYERS, D_MODEL, 2 * D_INNER), D_MODEL ** -0.5),
        "cm_v_gain": 1.0 + nrm(ks[14], (N_CHUNK_LAYERS, D_INNER), 0.1),
        "cm_w_s": nrm(ks[15], (N_CHUNK_LAYERS, N_SGU_GROUPS, CHUNK, CHUNK), CHUNK ** -0.5),
        "cm_b_s": 1.0 + nrm(ks[16], (N_CHUNK_LAYERS, N_SGU_GROUPS, CHUNK), 0.1),
        "cm_w_out": nrm(ks[17], (N_CHUNK_LAYERS, D_INNER, D_MODEL), D_INNER ** -0.5),
        "moe_w_group": nrm(ks[18], (DEPTH, D_MODEL, N_EXPERT_GROUPS), D_MODEL ** -0.5),
        "moe_b_group": nrm(ks[19], (DEPTH, N_EXPERT_GROUPS), 0.01),
        "moe_w_expert": nrm(ks[20], (DEPTH, D_MODEL, N_EXPERTS), D_MODEL ** -0.5),
        "moe_b_expert": nrm(ks[21], (DEPTH, N_EXPERTS), 0.01),
        "moe_w_gate_up": nrm(ks[22], (DEPTH, N_EXPERTS, D_MODEL, 2 * D_EXPERT), D_MODEL ** -0.5),
        "moe_w_down": nrm(ks[23], (DEPTH, N_EXPERTS, D_EXPERT, D_MODEL), D_EXPERT ** -0.5),
    }


def reference(x_prompt, x_sample, cache_k, cache_v, c, c_ctx, norm_gain, w_mod, b_mod,
              attn_w_qkv, attn_q_gain, attn_k_gain, attn_w_o,
              cm_w_in, cm_v_gain, cm_w_s, cm_b_s, cm_w_out,
              moe_w_group, moe_b_group, moe_w_expert, moe_b_expert, moe_w_gate_up, moe_w_down):
    x_ctx = x_prompt
    x_lat = x_sample
    new_k, new_v = [], []
    for i in range(DEPTH):
        m_ctx = modulation(c_ctx[None, :], w_mod[i], b_mod[i])
        m_lat = modulation(c, w_mod[i], b_mod[i])
        h_ctx = adaln(x_ctx, norm_gain[i, 0], m_ctx[:, 0], m_ctx[:, 1])
        h_lat = adaln(x_lat, norm_gain[i, 0], m_lat[:, 0], m_lat[:, 1])
        if i % N_MIXERS == 0:
            a = i // N_MIXERS
            q, k, v = qkv_proj(h_ctx, attn_w_qkv[a], attn_q_gain[a], attn_k_gain[a])
            o_ctx = block_attention(q, k, v) @ attn_w_o[a]
            new_k.append(k)
            new_v.append(v)
            ql, kl, vl = qkv_proj(h_lat, attn_w_qkv[a], attn_q_gain[a], attn_k_gain[a])
            ql = axial_rope(ql)
            kl = axial_rope(kl)
            keys = jnp.concatenate([cache_k[:, a].astype(kl.dtype), kl], axis=1)
            vals = jnp.concatenate([cache_v[:, a].astype(vl.dtype), vl], axis=1)
            o_lat = block_attention(ql, keys, vals) @ attn_w_o[a]
        else:
            b = i // N_MIXERS
            o_ctx = chunk_mlp(h_ctx, cm_w_in[b], cm_v_gain[b], cm_w_s[b], cm_b_s[b], cm_w_out[b])
            o_lat = chunk_mlp(h_lat, cm_w_in[b], cm_v_gain[b], cm_w_s[b], cm_b_s[b], cm_w_out[b])
        x_ctx = x_ctx + m_ctx[:, 2] * o_ctx
        x_lat = x_lat + m_lat[:, 2] * o_lat
        h_ctx = adaln(x_ctx, norm_gain[i, 1], m_ctx[:, 3], m_ctx[:, 4])
        h_lat = adaln(x_lat, norm_gain[i, 1], m_lat[:, 3], m_lat[:, 4])
        f_ctx = hier_moe(h_ctx, moe_w_group[i], moe_b_group[i], moe_w_expert[i], moe_b_expert[i],
                         moe_w_gate_up[i], moe_w_down[i])
        f_lat = hier_moe(h_lat, moe_w_group[i], moe_b_group[i], moe_w_expert[i], moe_b_expert[i],
                         moe_w_gate_up[i], moe_w_down[i])
        x_ctx = x_ctx + m_ctx[:, 5] * f_ctx
        x_lat = x_lat + m_lat[:, 5] * f_lat
    y_prompt = x_ctx
    y_sample = x_lat
    new_cache_k = jnp.stack(new_k, axis=1)
    new_cache_v = jnp.stack(new_v, axis=1)
    return (y_prompt, y_sample, new_cache_k, new_cache_v)
```

```python
import functools

import jax
import jax.numpy as jnp
from jax import lax
from jax.experimental import pallas as pl
from jax.experimental.pallas import tpu as pltpu

f32 = jnp.float32
bf16 = jnp.bfloat16
i32 = jnp.int32
u32 = jnp.uint32

D_MODEL = 4096
BATCH, SEQ = 16, 256
DEC_BATCH, DEC_SEQ = 4, 2048
PAST_LEN = 256
DEPTH = 2
GRID_W = 64
N_HEADS, N_KV_HEADS, HEAD_DIM = 32, 8, 128
GQA = N_HEADS // N_KV_HEADS
Q_DIM = N_HEADS * HEAD_DIM
KV_DIM = N_KV_HEADS * HEAD_DIM
QKV_DIM = Q_DIM + 2 * KV_DIM
AXIS_DIM = HEAD_DIM // 2
ROPE_THETA = 10000.0
CHUNK = 128
D_INNER = D_MODEL
N_SGU_GROUPS = 8
SGU_GROUP_DIM = D_INNER // N_SGU_GROUPS
N_EXPERT_GROUPS, EXPERTS_PER_GROUP = 4, 8
N_EXPERTS = N_EXPERT_GROUPS * EXPERTS_PER_GROUP
TOP_K = 2
D_EXPERT = D_MODEL // 4
N_MOD = 6
RMS_EPS = 1e-6

N_CTX = BATCH * SEQ
N_LAT = DEC_BATCH * DEC_SEQ
N_TOK = N_CTX + N_LAT
N_COPIES = N_TOK * TOP_K
MOD_ROWS = 8

LANES = 128
HALF_D = D_MODEL // 2

TM = 512
TN_QKV = 512
TN_PROJ = 1024
TN_MOD = 512
TQ_LAT = 256
TE = 256
N_ETILES = N_COPIES // TE + N_EXPERTS
R_MAX = N_ETILES * TE
TN_GU = 256
TN_DOWN = 1024
TC = 256
ROUTER_LANES = 128
EXPERT_LANE0 = 32

VMEM_LIMIT = 56 * 1024 * 1024


def _params(n_axes):
    return pltpu.CompilerParams(dimension_semantics=("arbitrary",) * n_axes, vmem_limit_bytes=VMEM_LIMIT)


def _mod_row(i, tm):
    start = i * tm
    return jnp.where(start < N_CTX, 0, 1 + (start - N_CTX) // DEC_SEQ)


def _mod_spec(layer, k, tm, tn=D_MODEL, col=None, row_fn=None):
    row_fn = row_fn or (lambda i: _mod_row(i, tm))
    if col is None:
        return pl.BlockSpec((None, None, None, 1, tn), lambda i, j: (layer, row_fn(i), k, 0, 0))
    return pl.BlockSpec((None, None, None, 1, tn), lambda i, j: (layer, row_fn(i), k, 0, j))


def _adaln(x, gain, shift, scale):
    ms = jnp.mean(x * x, axis=-1, keepdims=True)
    return x * lax.rsqrt(ms + RMS_EPS) * gain * (1.0 + scale) + shift


def _mod_kernel(c_ref, w_ref, b_ref, o_ref):
    s = jax.nn.silu(c_ref[...]).astype(bf16)
    o_ref[...] = jnp.dot(s, w_ref[...].astype(bf16), preferred_element_type=f32) + b_ref[...]


def _modulation(cond, w_mod, b_mod):
    n = N_MOD * D_MODEL
    out = pl.pallas_call(
        _mod_kernel,
        grid=(DEPTH, n // TN_MOD),
        in_specs=[
            pl.BlockSpec((MOD_ROWS, D_MODEL), lambda l, j: (0, 0)),
            pl.BlockSpec((None, D_MODEL, TN_MOD), lambda l, j: (l, 0, j)),
            pl.BlockSpec((None, 1, TN_MOD), lambda l, j: (l, 0, j)),
        ],
        out_specs=pl.BlockSpec((None, MOD_ROWS, TN_MOD), lambda l, j: (l, 0, j)),
        out_shape=jax.ShapeDtypeStruct((DEPTH, MOD_ROWS, n), f32),
        compiler_params=_params(2),
        name="modulation",
    )(cond, w_mod, b_mod.reshape(DEPTH, 1, n))
    return out.reshape(DEPTH, MOD_ROWS, N_MOD, 1, D_MODEL)


def _qkv_kernel(*refs, rope, with_cache):
    x_ref, g_ref, sh_ref, sc_ref, w_ref, qg_ref, kg_ref = refs[:7]
    pos = 7
    if rope:
        cos_ref, sa_ref, sb_ref = refs[pos:pos + 3]
        pos += 3
    o_ref = refs[pos]
    pos += 1
    if with_cache:
        kv_ref = refs[pos]
        pos += 1
    h_scr = refs[pos]

    j = pl.program_id(1)
    nq = Q_DIM // TN_QKV
    nk = KV_DIM // TN_QKV

    @pl.when(j == 0)
    def _():
        h_scr[...] = _adaln(x_ref[...], g_ref[...], sh_ref[...], sc_ref[...]).astype(bf16)

    acc = jnp.dot(h_scr[...], w_ref[...], preferred_element_type=f32)

    def normed_head(hh, gain):
        a = acc[:, hh * HEAD_DIM:(hh + 1) * HEAD_DIM]
        y = a * lax.rsqrt(jnp.mean(a * a, axis=-1, keepdims=True) + RMS_EPS) * gain
        if rope:
            y = (y * cos_ref[...] + pltpu.roll(y, HEAD_DIM - AXIS_DIM // 2, 1) * sa_ref[...]
                 + pltpu.roll(y, AXIS_DIM // 2, 1) * sb_ref[...])
        return y

    @pl.when(j < nq)
    def _():
        gain = qg_ref[...] * (HEAD_DIM ** -0.5)
        for hh in range(TN_QKV // HEAD_DIM):
            o_ref[:, hh * HEAD_DIM:(hh + 1) * HEAD_DIM] = normed_head(hh, gain).astype(bf16)

    @pl.when((j >= nq) & (j < nq + nk))
    def _():
        for hh in range(TN_QKV // HEAD_DIM):
            y = normed_head(hh, kg_ref[...])
            o_ref[:, hh * HEAD_DIM:(hh + 1) * HEAD_DIM] = y.astype(bf16)
            if with_cache:
                kv_ref[:, hh * HEAD_DIM:(hh + 1) * HEAD_DIM] = y

    @pl.when(j >= nq + nk)
    def _():
        o_ref[...] = acc.astype(bf16)
        if with_cache:
            kv_ref[...] = acc


def _qkv_proj(x, mod, layer, gain, w_qkv, q_gain, k_gain, *, tok0, n_tok, rope_tabs=None, with_cache=False):
    blk0 = tok0 // TM
    nj = QKV_DIM // TN_QKV
    nq = Q_DIM // TN_QKV
    row_fn = lambda i: _mod_row(i + blk0, TM)
    in_specs = [
        pl.BlockSpec((TM, D_MODEL), lambda i, j: (i + blk0, 0)),
        pl.BlockSpec((1, D_MODEL), lambda i, j: (0, 0)),
        _mod_spec(layer, 0, TM, row_fn=row_fn),
        _mod_spec(layer, 1, TM, row_fn=row_fn),
        pl.BlockSpec((D_MODEL, TN_QKV), lambda i, j: (0, j)),
        pl.BlockSpec((1, HEAD_DIM), lambda i, j: (0, 0)),
        pl.BlockSpec((1, HEAD_DIM), lambda i, j: (0, 0)),
    ]
    args = [x, gain, mod, mod, w_qkv, q_gain, k_gain]
    if rope_tabs is not None:
        nb = DEC_SEQ // TM
        in_specs += [pl.BlockSpec((TM, HEAD_DIM), lambda i, j: (i % nb, 0))] * 3
        args += list(rope_tabs)
    out_specs = [pl.BlockSpec((TM, TN_QKV), lambda i, j: (i, j))]
    out_shape = [jax.ShapeDtypeStruct((n_tok, QKV_DIM), bf16)]
    if with_cache:
        out_specs.append(pl.BlockSpec((TM, TN_QKV), lambda i, j: (i, jnp.maximum(j - nq, 0))))
        out_shape.append(jax.ShapeDtypeStruct((n_tok, 2 * KV_DIM), f32))
    return pl.pallas_call(
        functools.partial(_qkv_kernel, rope=rope_tabs is not None, with_cache=with_cache),
        grid=(n_tok // TM, nj),
        in_specs=in_specs,
        out_specs=out_specs,
        out_shape=out_shape,
        scratch_shapes=[pltpu.VMEM((TM, D_MODEL), bf16)],
        compiler_params=_params(2),
        name="qkv_rope" if rope_tabs is not None else "qkv_ctx",
    )(*args)


def _rope_tables():
    t = jnp.arange(DEC_SEQ)
    row = (t // GRID_W).astype(f32)
    col = (t % GRID_W).astype(f32)
    inv_freq = ROPE_THETA ** (-jnp.arange(0, AXIS_DIM, 2, dtype=f32) / AXIS_DIM)
    lane = jnp.arange(HEAD_DIM)
    freq = inv_freq[lane % (AXIS_DIM // 2)]
    pos = jnp.where(lane[None, :] < AXIS_DIM, row[:, None], col[:, None])
    ang = pos * freq[None, :]
    first = (lane % AXIS_DIM) < (AXIS_DIM // 2)
    sin = jnp.sin(ang)
    return jnp.cos(ang), jnp.where(first[None, :], -sin, 0.0), jnp.where(first[None, :], 0.0, sin)


def _attn_kernel(*refs, with_cache, tq):
    q_ref, k_ref, v_ref = refs[:3]
    if with_cache:
        ck_ref, cv_ref, o_ref = refs[3:6]
    else:
        o_ref = refs[3]
    q = q_ref[...]
    qs = jnp.concatenate([q[:, g * HEAD_DIM:(g + 1) * HEAD_DIM] for g in range(GQA)], axis=0)
    dims = (((1,), (1,)), ((), ()))
    s = lax.dot_general(qs, k_ref[...], dims, preferred_element_type=f32)
    m = jnp.max(s, axis=-1, keepdims=True)
    if with_cache:
        sc = lax.dot_general(qs, ck_ref[...].astype(bf16), dims, preferred_element_type=f32)
        m = jnp.maximum(m, jnp.max(sc, axis=-1, keepdims=True))
    e = jnp.exp(s - m)
    l = jnp.sum(e, axis=-1, keepdims=True)
    acc = jnp.dot(e.astype(bf16), v_ref[...], preferred_element_type=f32)
    if with_cache:
        ec = jnp.exp(sc - m)
        l = l + jnp.sum(ec, axis=-1, keepdims=True)
        acc = acc + jnp.dot(ec.astype(bf16), cv_ref[...].astype(bf16), preferred_element_type=f32)
    o = acc / l
    for g in range(GQA):
        o_ref[:, g * HEAD_DIM:(g + 1) * HEAD_DIM] = o[g * tq:(g + 1) * tq].astype(bf16)


def _attention(qkv, *, n_batch, seq, tq, cache=None):
    nqb = seq // tq
    kcol = Q_DIM // HEAD_DIM
    vcol = (Q_DIM + KV_DIM) // HEAD_DIM
    in_specs = [
        pl.BlockSpec((tq, GQA * HEAD_DIM), lambda b, h, i: (b * nqb + i, h)),
        pl.BlockSpec((seq, HEAD_DIM), lambda b, h, i: (b, kcol + h)),
        pl.BlockSpec((seq, HEAD_DIM), lambda b, h, i: (b, vcol + h)),
    ]
    args = [qkv, qkv, qkv]
    if cache is not None:
        in_specs += [pl.BlockSpec((None, PAST_LEN, HEAD_DIM), lambda b, h, i: (b, 0, h))] * 2
        args += list(cache)
    return pl.pallas_call(
        functools.partial(_attn_kernel, with_cache=cache is not None, tq=tq),
        grid=(n_batch, N_KV_HEADS, nqb),
        in_specs=in_specs,
        out_specs=pl.BlockSpec((tq, GQA * HEAD_DIM), lambda b, h, i: (b * nqb + i, h)),
        out_shape=jax.ShapeDtypeStruct((n_batch * seq, Q_DIM), bf16),
        compiler_params=_params(3),
        name="attn_lat" if cache is not None else "attn_ctx",
    )(*args)


def _proj_res_kernel(a_ref, w_ref, x_ref, gate_ref, o_ref):
    o_ref[...] = x_ref[...] + gate_ref[...] * jnp.dot(a_ref[...], w_ref[...], preferred_element_type=f32)


def _proj_res(a, w, x, mod, layer):
    return pl.pallas_call(
        _proj_res_kernel,
        grid=(N_TOK // TM, D_MODEL // TN_PROJ),
        in_specs=[
            pl.BlockSpec((TM, a.shape[1]), lambda i, j: (i, 0)),
            pl.BlockSpec((a.shape[1], TN_PROJ), lambda i, j: (0, j)),
            pl.BlockSpec((TM, TN_PROJ), lambda i, j: (i, j)),
            _mod_spec(layer, 2, TM, tn=TN_PROJ, col=True),
        ],
        out_specs=pl.BlockSpec((TM, TN_PROJ), lambda i, j: (i, j)),
        out_shape=jax.ShapeDtypeStruct((N_TOK, D_MODEL), f32),
        compiler_params=_params(2),
        name="proj_res",
    )(a, w, x, mod)


def _win_kernel(x_ref, g_ref, sh_ref, sc_ref, w_ref, o_ref, ssq_ref, h_scr, *, tn):
    j = pl.program_id(1)
    nu = D_INNER // tn

    @pl.when(j == 0)
    def _():
        h_scr[...] = _adaln(x_ref[...], g_ref[...], sh_ref[...], sc_ref[...]).astype(bf16)

    y = jax.nn.gelu(jnp.dot(h_scr[...], w_ref[...], preferred_element_type=f32))
    o_ref[...] = y.astype(bf16)

    @pl.when(j == nu)
    def _():
        ssq_ref[...] = jnp.zeros_like(ssq_ref)

    @pl.when(j >= nu)
    def _():
        y2 = y * y
        part = y2[:, 0:LANES]
        for c in range(1, tn // LANES):
            part = part + y2[:, c * LANES:(c + 1) * LANES]
        ssq_ref[...] += part


def _chunk_in(x, mod, layer, gain, w_in):
    tn = TN_QKV
    return pl.pallas_call(
        functools.partial(_win_kernel, tn=tn),
        grid=(N_TOK // TM, 2 * D_INNER // tn),
        in_specs=[
            pl.BlockSpec((TM, D_MODEL), lambda i, j: (i, 0)),
            pl.BlockSpec((1, D_MODEL), lambda i, j: (0, 0)),
            _mod_spec(layer, 0, TM),
            _mod_spec(layer, 1, TM),
            pl.BlockSpec((D_MODEL, tn), lambda i, j: (0, j)),
        ],
        out_specs=[
            pl.BlockSpec((TM, tn), lambda i, j: (i, j)),
            pl.BlockSpec((TM, LANES), lambda i, j: (i, 0)),
        ],
        out_shape=[jax.ShapeDtypeStruct((N_TOK, 2 * D_INNER), bf16), jax.ShapeDtypeStruct((N_TOK, LANES), f32)],
        scratch_shapes=[pltpu.VMEM((TM, D_MODEL), bf16)],
        compiler_params=_params(2),
        name="chunk_in",
    )(x, gain, mod, mod, w_in)


def _sgu_out_kernel(u_ref, v_ref, ssq_ref, vg_ref, ws_ref, bs_ref, w_ref, x_ref, gate_ref, o_ref, a_scr):
    j = pl.program_id(1)

    @pl.when(j == 0)
    def _():
        rstd = lax.rsqrt(jnp.sum(ssq_ref[...], axis=-1, keepdims=True) * (1.0 / D_INNER) + RMS_EPS)
        for c in range(TM // CHUNK):
            rows = slice(c * CHUNK, (c + 1) * CHUNK)
            rs = rstd[rows]
            for g in range(N_SGU_GROUPS):
                cols = slice(g * SGU_GROUP_DIM, (g + 1) * SGU_GROUP_DIM)
                vn = (v_ref[rows, cols].astype(f32) * rs * vg_ref[:, cols]).astype(bf16)
                vm = jnp.dot(ws_ref[g].astype(bf16), vn, preferred_element_type=f32) + bs_ref[g]
                a_scr[rows, cols] = (u_ref[rows, cols].astype(f32) * vm).astype(bf16)

    o_ref[...] = x_ref[...] + gate_ref[...] * jnp.dot(a_scr[...], w_ref[...], preferred_element_type=f32)


def _chunk_out(uv, ssq, v_gain, w_s, b_s, w_out, x, mod, layer):
    return pl.pallas_call(
        _sgu_out_kernel,
        grid=(N_TOK // TM, D_MODEL // TN_PROJ),
        in_specs=[
            pl.BlockSpec((TM, D_INNER), lambda i, j: (i, 0)),
            pl.BlockSpec((TM, D_INNER), lambda i, j: (i, 1)),
            pl.BlockSpec((TM, LANES), lambda i, j: (i, 0)),
            pl.BlockSpec((1, D_INNER), lambda i, j: (0, 0)),
            pl.BlockSpec((N_SGU_GROUPS, CHUNK, CHUNK), lambda i, j: (0, 0, 0)),
            pl.BlockSpec((N_SGU_GROUPS, CHUNK, 1), lambda i, j: (0, 0, 0)),
            pl.BlockSpec((D_INNER, TN_PROJ), lambda i, j: (0, j)),
            pl.BlockSpec((TM, TN_PROJ), lambda i, j: (i, j)),
            _mod_spec(layer, 2, TM, tn=TN_PROJ, col=True),
        ],
        out_specs=pl.BlockSpec((TM, TN_PROJ), lambda i, j: (i, j)),
        out_shape=jax.ShapeDtypeStruct((N_TOK, D_MODEL), f32),
        scratch_shapes=[pltpu.VMEM((TM, D_INNER), bf16)],
        compiler_params=_params(2),
        name="sgu_out",
    )(uv, uv, ssq, v_gain, w_s, b_s, w_out, x, mod)


def _router_kernel(x_ref, g_ref, sh_ref, sc_ref, whi_ref, wlo_ref, b_ref, hp_ref, r_ref):
    h = _adaln(x_ref[...], g_ref[...], sh_ref[...], sc_ref[...])
    hb = h.astype(bf16)
    hf = hb.astype(f32)
    lo = (h - hf).astype(bf16)
    logits = (jnp.dot(hb, whi_ref[...], preferred_element_type=f32)
              + (jnp.dot(lo, whi_ref[...], preferred_element_type=f32)
                 + jnp.dot(hb, wlo_ref[...], preferred_element_type=f32))) + b_ref[...]

    bits = lax.bitcast_convert_type(hf, u32)
    hp_ref[...] = (bits[:, :HALF_D] >> 16) | bits[:, HALF_D:]

    lane = lax.broadcasted_iota(i32, logits.shape, 1).astype(f32)
    neg = -jnp.inf
    big = float(ROUTER_LANES)
    lg = jnp.where(lane < N_EXPERT_GROUPS, logits, neg)
    mg = jnp.max(lg, axis=-1, keepdims=True)
    pg_top = 1.0 / jnp.sum(jnp.exp(lg - mg), axis=-1, keepdims=True)
    grp = jnp.min(jnp.where(lg == mg, lane, big), axis=-1, keepdims=True)
    lane0 = EXPERT_LANE0 + EXPERTS_PER_GROUP * grp
    le = jnp.where((lane >= lane0) & (lane < lane0 + EXPERTS_PER_GROUP), logits, neg)
    m1 = jnp.max(le, axis=-1, keepdims=True)
    i1 = jnp.min(jnp.where(le == m1, lane, big), axis=-1, keepdims=True)
    le2 = jnp.where(lane == i1, neg, le)
    m2 = jnp.max(le2, axis=-1, keepdims=True)
    i2 = jnp.min(jnp.where(le2 == m2, lane, big), axis=-1, keepdims=True)
    e2 = jnp.exp(m2 - m1)
    p1 = 1.0 / (1.0 + e2)
    p2 = e2 * p1
    r_ref[...] = jnp.where(lane == 0, i1 - EXPERT_LANE0,
                           jnp.where(lane == 1, i2 - EXPERT_LANE0,
                                     jnp.where(lane == 2, pg_top * p1,
                                               jnp.where(lane == 3, pg_top * p2, 0.0))))


def _router(x, mod, layer, gain, w_group, b_group, w_expert, b_expert):
    tm = 256
    pad0 = jnp.zeros((D_MODEL, EXPERT_LANE0 - N_EXPERT_GROUPS), f32)
    pad1 = jnp.zeros((D_MODEL, ROUTER_LANES - EXPERT_LANE0 - N_EXPERTS), f32)
    w_r = jnp.concatenate([w_group, pad0, w_expert, pad1], axis=1)
    w_hi = w_r.astype(bf16)
    w_lo = (w_r - w_hi.astype(f32)).astype(bf16)
    b_r = jnp.concatenate([b_group, pad0[0], b_expert, pad1[0]])[None, :]
    return pl.pallas_call(
        _router_kernel,
        grid=(N_TOK // tm, 1),
        in_specs=[
            pl.BlockSpec((tm, D_MODEL), lambda i, j: (i, 0)),
            pl.BlockSpec((1, D_MODEL), lambda i, j: (0, 0)),
            _mod_spec(layer, 3, tm),
            _mod_spec(layer, 4, tm),
            pl.BlockSpec((D_MODEL, ROUTER_LANES), lambda i, j: (0, 0)),
            pl.BlockSpec((D_MODEL, ROUTER_LANES), lambda i, j: (0, 0)),
            pl.BlockSpec((1, ROUTER_LANES), lambda i, j: (0, 0)),
        ],
        out_specs=[
            pl.BlockSpec((tm, HALF_D), lambda i, j: (i, 0)),
            pl.BlockSpec((tm, ROUTER_LANES), lambda i, j: (i, 0)),
        ],
        out_shape=[jax.ShapeDtypeStruct((N_TOK, HALF_D), u32), jax.ShapeDtypeStruct((N_TOK, ROUTER_LANES), f32)],
        compiler_params=_params(2),
        name="router",
    )(x, gain, mod, mod, w_hi, w_lo, b_r)


def _moe_layout(route):
    flat_e = route[:, 0:TOP_K].astype(i32).reshape(-1)
    onehot = (flat_e[:, None] == jnp.arange(N_EXPERTS, dtype=i32)[None, :]).astype(i32)
    counts = jnp.sum(onehot, axis=0)
    ustart = jnp.cumsum(counts) - counts
    pcounts = ((counts + TE - 1) // TE) * TE
    pend = jnp.cumsum(pcounts)
    pstart = pend - pcounts
    n_used = pend[-1] // TE
    tiles = jnp.arange(N_ETILES, dtype=i32)
    tile_e = jnp.sum((tiles[:, None] * TE >= pend[None, :]).astype(i32), axis=1)
    last_e = jnp.max(jnp.where(tiles < n_used, tile_e, 0))
    tile_e = jnp.minimum(tile_e, last_e)
    tile_blk = jnp.minimum(tiles, n_used - 1)
    rank = jnp.sum(onehot * (jnp.cumsum(onehot, axis=0) - 1), axis=1)
    pos = jnp.sum(onehot * pstart[None, :], axis=1) + rank
    order = jnp.argsort(flat_e, stable=True).astype(i32)
    rows = jnp.arange(R_MAX, dtype=i32)
    row_e = tile_e[rows // TE]
    off = rows - pstart[row_e]
    valid = (off < counts[row_e]) & (rows < n_used * TE)
    src = order[jnp.clip(ustart[row_e] + off, 0, N_COPIES - 1)] // TOP_K
    src_tok = jnp.where(valid, src, 0)
    return tile_e, tile_blk, n_used.reshape(1).astype(i32), src_tok, pos.astype(i32)


def _gather_kernel(src_ref, nu_ref, hp_hbm, o_ref, sem):
    t = pl.program_id(0)

    @pl.when(t < nu_ref[0])
    def _():
        base = t * TE

        def issue(r, carry):
            tok = src_ref[base + r]
            pltpu.make_async_copy(hp_hbm.at[pl.ds(tok, 1)], o_ref.at[pl.ds(r, 1)], sem).start()
            return carry

        lax.fori_loop(0, TE, issue, 0)
        pltpu.make_async_copy(hp_hbm.at[pl.ds(0, TE)], o_ref, sem).wait()


def _dispatch(hp, src_tok, n_used):
    grid_spec = pltpu.PrefetchScalarGridSpec(
        num_scalar_prefetch=2,
        grid=(N_ETILES,),
        in_specs=[pl.BlockSpec(memory_space=pl.ANY)],
        out_specs=pl.BlockSpec((TE, HALF_D), lambda t, src, nu: (jnp.minimum(t, nu[0] - 1), 0)),
        scratch_shapes=[pltpu.SemaphoreType.DMA(())],
    )
    return pl.pallas_call(
        _gather_kernel,
        grid_spec=grid_spec,
        out_shape=jax.ShapeDtypeStruct((R_MAX, HALF_D), u32),
        compiler_params=_params(1),
        name="dispatch",
    )(src_tok, n_used, hp)


def _gate_up_kernel(te_ref, tb_ref, nu_ref, xs_ref, wg_ref, wu_ref, o_ref, wb_scr):
    t = pl.program_id(1)
    new_expert = (t == 0) | (te_ref[t] != te_ref[jnp.maximum(t - 1, 0)])

    @pl.when(new_expert)
    def _():
        wb_scr[:, 0:TN_GU] = wg_ref[...].astype(bf16)
        wb_scr[:, TN_GU:2 * TN_GU] = wu_ref[...].astype(bf16)

    @pl.when(t < nu_ref[0])
    def _():
        w = xs_ref[...]
        lo = lax.bitcast_convert_type(w << 16, f32).astype(bf16)
        hi = lax.bitcast_convert_type(w & jnp.uint32(0xFFFF0000), f32).astype(bf16)
        acc = (jnp.dot(lo, wb_scr[0:HALF_D, :], preferred_element_type=f32)
               + jnp.dot(hi, wb_scr[HALF_D:D_MODEL, :], preferred_element_type=f32))
        o_ref[...] = (jax.nn.silu(acc[:, 0:TN_GU]) * acc[:, TN_GU:2 * TN_GU]).astype(bf16)


def _gate_up(xs, w_gate_up, tile_e, tile_blk, n_used):
    nj = D_EXPERT // TN_GU
    grid_spec = pltpu.PrefetchScalarGridSpec(
        num_scalar_prefetch=3,
        grid=(nj, N_ETILES),
        in_specs=[
            pl.BlockSpec((TE, HALF_D), lambda j, t, te, tb, nu: (tb[t], 0)),
            pl.BlockSpec((None, D_MODEL, TN_GU), lambda j, t, te, tb, nu: (te[t], 0, j)),
            pl.BlockSpec((None, D_MODEL, TN_GU), lambda j, t, te, tb, nu: (te[t], 0, nj + j)),
        ],
        out_specs=pl.BlockSpec((TE, TN_GU), lambda j, t, te, tb, nu: (tb[t], j)),
        scratch_shapes=[pltpu.VMEM((D_MODEL, 2 * TN_GU), bf16)],
    )
    return pl.pallas_call(
        _gate_up_kernel,
        grid_spec=grid_spec,
        out_shape=jax.ShapeDtypeStruct((R_MAX, D_EXPERT), bf16),
        compiler_params=_params(2),
        name="moe_gate_up",
    )(tile_e, tile_blk, n_used, xs, w_gate_up, w_gate_up)


def _down_kernel(te_ref, tb_ref, nu_ref, h_ref, w_ref, o_ref, wb_scr):
    t = pl.program_id(1)
    new_expert = (t == 0) | (te_ref[t] != te_ref[jnp.maximum(t - 1, 0)])

    @pl.when(new_expert)
    def _():
        wb_scr[...] = w_ref[...].astype(bf16)

    @pl.when(t < nu_ref[0])
    def _():
        o_ref[...] = jnp.dot(h_ref[...], wb_scr[...], preferred_element_type=f32)


def _down(h1, w_down, tile_e, tile_blk, n_used):
    grid_spec = pltpu.PrefetchScalarGridSpec(
        num_scalar_prefetch=3,
        grid=(D_MODEL // TN_DOWN, N_ETILES),
        in_specs=[
            pl.BlockSpec((TE, D_EXPERT), lambda j, t, te, tb, nu: (tb[t], 0)),
            pl.BlockSpec((None, D_EXPERT, TN_DOWN), lambda j, t, te, tb, nu: (te[t], 0, j)),
        ],
        out_specs=pl.BlockSpec((TE, TN_DOWN), lambda j, t, te, tb, nu: (tb[t], j)),
        scratch_shapes=[pltpu.VMEM((D_EXPERT, TN_DOWN), bf16)],
    )
    return pl.pallas_call(
        _down_kernel,
        grid_spec=grid_spec,
        out_shape=jax.ShapeDtypeStruct((R_MAX, D_MODEL), f32),
        compiler_params=_params(2),
        name="moe_down",
    )(tile_e, tile_blk, n_used, h1, w_down)


def _combine_kernel(pos_ref, x_ref, gate_ref, r_ref, ys_hbm, o_ref, buf, sem, *, tok0):
    base = (pl.program_id(0) * TC + tok0) * TOP_K

    def issue(r, carry):
        for k in range(TOP_K):
            p = pos_ref[base + TOP_K * r + k]
            pltpu.make_async_copy(ys_hbm.at[pl.ds(p, 1)], buf.at[k, pl.ds(r, 1)], sem).start()
        return carry

    lax.fori_loop(0, TC, issue, 0)
    for k in range(TOP_K):
        pltpu.make_async_copy(ys_hbm.at[pl.ds(0, TC)], buf.at[k], sem).wait()
    r = r_ref[...]
    f = r[:, 2:3] * buf[0] + r[:, 3:4] * buf[1]
    o_ref[...] = x_ref[...] + gate_ref[...] * f


def _combine(x, mod, layer, route, ys, pos, *, tok0, n_tok):
    blk0 = tok0 // TC
    grid_spec = pltpu.PrefetchScalarGridSpec(
        num_scalar_prefetch=1,
        grid=(n_tok // TC,),
        in_specs=[
            pl.BlockSpec((TC, D_MODEL), lambda i, pos: (i + blk0, 0)),
            pl.BlockSpec((None, None, None, 1, D_MODEL), lambda i, pos: (layer, _mod_row(i + blk0, TC), 5, 0, 0)),
            pl.BlockSpec((TC, ROUTER_LANES), lambda i, pos: (i + blk0, 0)),
            pl.BlockSpec(memory_space=pl.ANY),
        ],
        out_specs=pl.BlockSpec((TC, D_MODEL), lambda i, pos: (i, 0)),
        scratch_shapes=[pltpu.VMEM((TOP_K, TC, D_MODEL), f32), pltpu.SemaphoreType.DMA(())],
    )
    return pl.pallas_call(
        functools.partial(_combine_kernel, tok0=tok0),
        grid_spec=grid_spec,
        out_shape=jax.ShapeDtypeStruct((n_tok, D_MODEL), f32),
        compiler_params=_params(1),
        name="combine",
    )(pos, x, mod, route, ys)


def _hier_moe(x, mod, layer, gain, w_group, b_group, w_expert, b_expert, w_gate_up, w_down, *, split):
    hp, route = _router(x, mod, layer, gain, w_group, b_group, w_expert, b_expert)
    tile_e, tile_blk, n_used, src_tok, pos = _moe_layout(route)
    xs = _dispatch(hp, src_tok, n_used)
    h1 = _gate_up(xs, w_gate_up, tile_e, tile_blk, n_used)
    ys = _down(h1, w_down, tile_e, tile_blk, n_used)
    if split:
        return (_combine(x, mod, layer, route, ys, pos, tok0=0, n_tok=N_CTX),
                _combine(x, mod, layer, route, ys, pos, tok0=N_CTX, n_tok=N_LAT))
    return _combine(x, mod, layer, route, ys, pos, tok0=0, n_tok=N_TOK)


def kernel(x_prompt, x_sample, cache_k, cache_v, c, c_ctx, norm_gain, w_mod, b_mod, attn_w_qkv, attn_q_gain, attn_k_gain, attn_w_o, cm_w_in, cm_v_gain, cm_w_s, cm_b_s, cm_w_out, moe_w_group, moe_b_group, moe_w_expert, moe_b_expert, moe_w_gate_up, moe_w_down):
    x = jnp.concatenate([x_prompt.reshape(N_CTX, D_MODEL), x_sample.reshape(N_LAT, D_MODEL)], axis=0)
    cond = jnp.concatenate([c_ctx[None, :], c, jnp.zeros((MOD_ROWS - 1 - DEC_BATCH, D_MODEL), f32)], axis=0)
    mod = _modulation(cond, w_mod, b_mod)

    w_qkv = attn_w_qkv[0].astype(bf16)
    q_gain = attn_q_gain[0][None, :]
    k_gain = attn_k_gain[0][None, :]
    gain0 = norm_gain[0, 0][None, :]
    qkv_ctx, kv_new = _qkv_proj(x, mod, 0, gain0, w_qkv, q_gain, k_gain, tok0=0, n_tok=N_CTX, with_cache=True)
    (qkv_lat,) = _qkv_proj(x, mod, 0, gain0, w_qkv, q_gain, k_gain, tok0=N_CTX, n_tok=N_LAT,
                           rope_tabs=_rope_tables())
    o_ctx = _attention(qkv_ctx, n_batch=BATCH, seq=SEQ, tq=SEQ)
    ck = cache_k[:, 0].reshape(DEC_BATCH, PAST_LEN, KV_DIM)
    cv = cache_v[:, 0].reshape(DEC_BATCH, PAST_LEN, KV_DIM)
    o_lat = _attention(qkv_lat, n_batch=DEC_BATCH, seq=DEC_SEQ, tq=TQ_LAT, cache=(ck, cv))
    o = jnp.concatenate([o_ctx, o_lat], axis=0)
    x = _proj_res(o, attn_w_o[0].astype(bf16), x, mod, 0)
    x = _hier_moe(x, mod, 0, norm_gain[0, 1][None, :], moe_w_group[0], moe_b_group[0], moe_w_expert[0],
                  moe_b_expert[0], moe_w_gate_up[0], moe_w_down[0], split=False)

    uv, ssq = _chunk_in(x, mod, 1, norm_gain[1, 0][None, :], cm_w_in[0].astype(bf16))
    x = _chunk_out(uv, ssq, cm_v_gain[0][None, :], cm_w_s[0], cm_b_s[0][:, :, None], cm_w_out[0].astype(bf16),
                   x, mod, 1)
    y_ctx, y_lat = _hier_moe(x, mod, 1, norm_gain[1, 1][None, :], moe_w_group[1], moe_b_group[1], moe_w_expert[1],
                             moe_b_expert[1], moe_w_gate_up[1], moe_w_down[1], split=True)

    new_k = kv_new[:, :KV_DIM].reshape(BATCH, 1, SEQ, N_KV_HEADS, HEAD_DIM)
    new_v = kv_new[:, KV_DIM:].reshape(BATCH, 1, SEQ, N_KV_HEADS, HEAD_DIM)
    return (y_ctx.reshape(BATCH, SEQ, D_MODEL), y_lat.reshape(DEC_BATCH, DEC_SEQ, D_MODEL), new_k, new_v)
```

```python
import functools

import jax
import jax.numpy as jnp
from jax import lax
from jax.experimental import pallas as pl
from jax.experimental.pallas import tpu as pltpu

f32 = jnp.float32
bf16 = jnp.bfloat16
i32 = jnp.int32
u32 = jnp.uint32

D_MODEL = 4096
BATCH, SEQ = 16, 256
DEC_BATCH, DEC_SEQ = 4, 2048
PAST_LEN = 256
DEPTH = 2
GRID_W = 64
N_HEADS, N_KV_HEADS, HEAD_DIM = 32, 8, 128
GQA = N_HEADS // N_KV_HEADS
Q_DIM = N_HEADS * HEAD_DIM
KV_DIM = N_KV_HEADS * HEAD_DIM
QKV_DIM = Q_DIM + 2 * KV_DIM
AXIS_DIM = HEAD_DIM // 2
ROPE_THETA = 10000.0
CHUNK = 128
D_INNER = D_MODEL
N_SGU_GROUPS = 8
SGU_GROUP_DIM = D_INNER // N_SGU_GROUPS
N_EXPERT_GROUPS, EXPERTS_PER_GROUP = 4, 8
N_EXPERTS = N_EXPERT_GROUPS * EXPERTS_PER_GROUP
TOP_K = 2
D_EXPERT = D_MODEL // 4
N_MOD = 6
RMS_EPS = 1e-6

N_CTX = BATCH * SEQ
N_LAT = DEC_BATCH * DEC_SEQ
N_TOK = N_CTX + N_LAT
N_COPIES = N_TOK * TOP_K
MOD_ROWS = 8

LANES = 128
HALF_D = D_MODEL // 2

TM = 512
TN_QKV = 1024
TN_PROJ = 1024
TN_MOD = 512
TQ = 256
KEY_CHUNK = 512
TE = 256
N_ETILES = N_COPIES // TE + N_EXPERTS
R_MAX = N_ETILES * TE
TN_GU = 512
TN_DOWN = 2048
TT = 512
TC = 256
LOG2E = 1.4426950408889634
ROUTER_LANES = 128
EXPERT_LANE0 = 32

VMEM_LIMIT = 56 * 1024 * 1024


def _params(n_axes):
    return pltpu.CompilerParams(dimension_semantics=("arbitrary",) * n_axes, vmem_limit_bytes=VMEM_LIMIT)


def _mod_row(i, tm):
    start = i * tm
    return jnp.where(start < N_CTX, 0, 1 + (start - N_CTX) // DEC_SEQ)


def _mod_spec(layer, k, tm, tn=D_MODEL, col=None, row_fn=None):
    row_fn = row_fn or (lambda i: _mod_row(i, tm))
    if col is None:
        return pl.BlockSpec((None, None, None, 1, tn), lambda i, j: (layer, row_fn(i), k, 0, 0))
    return pl.BlockSpec((None, None, None, 1, tn), lambda i, j: (layer, row_fn(i), k, 0, j))


def _adaln(x, gain, shift, scale):
    ms = jnp.mean(x * x, axis=-1, keepdims=True)
    return x * lax.rsqrt(ms + RMS_EPS) * gain * (1.0 + scale) + shift


def _mod_kernel(c_ref, w_ref, b_ref, o_ref):
    s = jax.nn.silu(c_ref[...]).astype(bf16)
    o_ref[...] = jnp.dot(s, w_ref[...].astype(bf16), preferred_element_type=f32) + b_ref[...]


def _modulation(cond, w_mod, b_mod):
    n = N_MOD * D_MODEL
    out = pl.pallas_call(
        _mod_kernel,
        grid=(DEPTH, n // TN_MOD),
        in_specs=[
            pl.BlockSpec((MOD_ROWS, D_MODEL), lambda l, j: (0, 0)),
            pl.BlockSpec((None, D_MODEL, TN_MOD), lambda l, j: (l, 0, j)),
            pl.BlockSpec((None, 1, TN_MOD), lambda l, j: (l, 0, j)),
        ],
        out_specs=pl.BlockSpec((None, MOD_ROWS, TN_MOD), lambda l, j: (l, 0, j)),
        out_shape=jax.ShapeDtypeStruct((DEPTH, MOD_ROWS, n), f32),
        compiler_params=_params(2),
        name="modulation",
    )(cond, w_mod, b_mod.reshape(DEPTH, 1, n))
    return out.reshape(DEPTH, MOD_ROWS, N_MOD, 1, D_MODEL)


def _qkv_kernel(*refs, rope, with_cache):
    x_ref, g_ref, sh_ref, sc_ref, w_ref, qg_ref, kg_ref = refs[:7]
    pos = 7
    if rope:
        cos_ref, sa_ref, sb_ref = refs[pos:pos + 3]
        pos += 3
    o_ref = refs[pos]
    pos += 1
    if with_cache:
        kv_ref = refs[pos]
        pos += 1
    h_scr = refs[pos]

    j = pl.program_id(1)
    nq = Q_DIM // TN_QKV
    nk = KV_DIM // TN_QKV

    @pl.when(j == 0)
    def _():
        h_scr[...] = _adaln(x_ref[...], g_ref[...], sh_ref[...], sc_ref[...]).astype(bf16)

    acc = jnp.dot(h_scr[...], w_ref[...], preferred_element_type=f32)

    def normed_head(hh, gain):
        a = acc[:, hh * HEAD_DIM:(hh + 1) * HEAD_DIM]
        y = a * lax.rsqrt(jnp.mean(a * a, axis=-1, keepdims=True) + RMS_EPS) * gain
        if rope:
            y = (y * cos_ref[...] + pltpu.roll(y, HEAD_DIM - AXIS_DIM // 2, 1) * sa_ref[...]
                 + pltpu.roll(y, AXIS_DIM // 2, 1) * sb_ref[...])
        return y

    @pl.when(j < nq)
    def _():
        gain = qg_ref[...] * (HEAD_DIM ** -0.5 * LOG2E)
        for hh in range(TN_QKV // HEAD_DIM):
            o_ref[:, hh * HEAD_DIM:(hh + 1) * HEAD_DIM] = normed_head(hh, gain).astype(bf16)

    @pl.when((j >= nq) & (j < nq + nk))
    def _():
        for hh in range(TN_QKV // HEAD_DIM):
            y = normed_head(hh, kg_ref[...])
            o_ref[:, hh * HEAD_DIM:(hh + 1) * HEAD_DIM] = y.astype(bf16)
            if with_cache:
                kv_ref[:, hh * HEAD_DIM:(hh + 1) * HEAD_DIM] = y

    @pl.when(j >= nq + nk)
    def _():
        o_ref[...] = acc.astype(bf16)
        if with_cache:
            kv_ref[...] = acc


def _qkv_proj(x, mod, layer, gain, w_qkv, q_gain, k_gain, *, tok0, rope_tabs=None, with_cache=False):
    n_tok = x.shape[0]
    blk0 = tok0 // TM
    nj = QKV_DIM // TN_QKV
    nq = Q_DIM // TN_QKV
    row_fn = lambda i: _mod_row(i + blk0, TM)
    in_specs = [
        pl.BlockSpec((TM, D_MODEL), lambda i, j: (i, 0)),
        pl.BlockSpec((1, D_MODEL), lambda i, j: (0, 0)),
        _mod_spec(layer, 0, TM, row_fn=row_fn),
        _mod_spec(layer, 1, TM, row_fn=row_fn),
        pl.BlockSpec((D_MODEL, TN_QKV), lambda i, j: (0, j)),
        pl.BlockSpec((1, HEAD_DIM), lambda i, j: (0, 0)),
        pl.BlockSpec((1, HEAD_DIM), lambda i, j: (0, 0)),
    ]
    args = [x, gain, mod, mod, w_qkv, q_gain, k_gain]
    if rope_tabs is not None:
        nb = DEC_SEQ // TM
        in_specs += [pl.BlockSpec((TM, HEAD_DIM), lambda i, j: (i % nb, 0))] * 3
        args += list(rope_tabs)
    out_specs = [pl.BlockSpec((TM, TN_QKV), lambda i, j: (i, j))]
    out_shape = [jax.ShapeDtypeStruct((n_tok, QKV_DIM), bf16)]
    if with_cache:
        out_specs.append(pl.BlockSpec((TM, TN_QKV), lambda i, j: (i, jnp.maximum(j - nq, 0))))
        out_shape.append(jax.ShapeDtypeStruct((n_tok, 2 * KV_DIM), f32))
    return pl.pallas_call(
        functools.partial(_qkv_kernel, rope=rope_tabs is not None, with_cache=with_cache),
        grid=(n_tok // TM, nj),
        in_specs=in_specs,
        out_specs=out_specs,
        out_shape=out_shape,
        scratch_shapes=[pltpu.VMEM((TM, D_MODEL), bf16)],
        compiler_params=_params(2),
        name="qkv_rope" if rope_tabs is not None else "qkv_ctx",
    )(*args)


def _rope_tables():
    t = jnp.arange(DEC_SEQ)
    row = (t // GRID_W).astype(f32)
    col = (t % GRID_W).astype(f32)
    inv_freq = ROPE_THETA ** (-jnp.arange(0, AXIS_DIM, 2, dtype=f32) / AXIS_DIM)
    lane = jnp.arange(HEAD_DIM)
    freq = inv_freq[lane % (AXIS_DIM // 2)]
    pos = jnp.where(lane[None, :] < AXIS_DIM, row[:, None], col[:, None])
    ang = pos * freq[None, :]
    first = (lane % AXIS_DIM) < (AXIS_DIM // 2)
    sin = jnp.sin(ang)
    return jnp.cos(ang), jnp.where(first[None, :], -sin, 0.0), jnp.where(first[None, :], 0.0, sin)


def _attn_kernel(*refs, with_cache):
    q_ref, k_ref, v_ref = refs[:3]
    if with_cache:
        ck_ref, cv_ref, o_ref = refs[3:6]
    else:
        o_ref = refs[3]
    dims = (((1,), (1,)), ((), ()))
    seq = k_ref.shape[0]
    kc = min(KEY_CHUNK, seq)
    chunks = []
    if with_cache:
        chunks.append((ck_ref[...].astype(bf16), cv_ref[...].astype(bf16)))
    for c in range(seq // kc):
        chunks.append((k_ref[c * kc:(c + 1) * kc, :], v_ref[c * kc:(c + 1) * kc, :]))
    for g in range(GQA):
        qg = q_ref[:, g * HEAD_DIM:(g + 1) * HEAD_DIM]
        m = l = acc = None
        for kk, vv in chunks:
            s = lax.dot_general(qg, kk, dims, preferred_element_type=f32)
            ms = jnp.max(s, axis=-1, keepdims=True)
            if m is None:
                m = ms
                p = jnp.exp2(s - m)
                l = jnp.sum(p, axis=-1, keepdims=True)
                acc = jnp.dot(p.astype(bf16), vv, preferred_element_type=f32)
            else:
                m_new = jnp.maximum(m, ms)
                alpha = jnp.exp2(m - m_new)
                p = jnp.exp2(s - m_new)
                l = alpha * l + jnp.sum(p, axis=-1, keepdims=True)
                acc = alpha * acc + jnp.dot(p.astype(bf16), vv, preferred_element_type=f32)
                m = m_new
        o_ref[:, g * HEAD_DIM:(g + 1) * HEAD_DIM] = (acc / l).astype(bf16)


def _attention(qkv, *, n_batch, seq, tq, cache=None):
    nqb = seq // tq
    kcol = Q_DIM // HEAD_DIM
    vcol = (Q_DIM + KV_DIM) // HEAD_DIM
    in_specs = [
        pl.BlockSpec((tq, GQA * HEAD_DIM), lambda b, h, i: (b * nqb + i, h)),
        pl.BlockSpec((seq, HEAD_DIM), lambda b, h, i: (b, kcol + h)),
        pl.BlockSpec((seq, HEAD_DIM), lambda b, h, i: (b, vcol + h)),
    ]
    args = [qkv, qkv, qkv]
    if cache is not None:
        in_specs += [pl.BlockSpec((None, PAST_LEN, HEAD_DIM), lambda b, h, i: (b, 0, h))] * 2
        args += list(cache)
    return pl.pallas_call(
        functools.partial(_attn_kernel, with_cache=cache is not None),
        grid=(n_batch, N_KV_HEADS, nqb),
        in_specs=in_specs,
        out_specs=pl.BlockSpec((tq, GQA * HEAD_DIM), lambda b, h, i: (b * nqb + i, h)),
        out_shape=jax.ShapeDtypeStruct((n_batch * seq, Q_DIM), bf16),
        compiler_params=_params(3),
        name="attn_lat" if cache is not None else "attn_ctx",
    )(*args)


def _proj_res_kernel(a_ref, w_ref, x_ref, gate_ref, *rest):
    o_ref = rest[-1]
    o_ref[...] = x_ref[...] + gate_ref[...] * jnp.dot(a_ref[...], w_ref[...], preferred_element_type=f32)


def _proj_res(a, w, x, mod, layer, *, tok0, out=None):
    n_tok = a.shape[0]
    blk0 = tok0 // TM
    row_fn = lambda i: _mod_row(i + blk0, TM)
    in_specs = [
        pl.BlockSpec((TM, a.shape[1]), lambda i, j: (i, 0)),
        pl.BlockSpec((a.shape[1], TN_PROJ), lambda i, j: (0, j)),
        pl.BlockSpec((TM, TN_PROJ), lambda i, j: (i, j)),
        _mod_spec(layer, 2, TM, tn=TN_PROJ, col=True, row_fn=row_fn),
    ]
    args = [a, w, x, mod]
    aliases = {}
    if out is not None:
        in_specs.append(pl.BlockSpec(memory_space=pl.ANY))
        args.append(out)
        aliases = {len(args) - 1: 0}
    return pl.pallas_call(
        _proj_res_kernel,
        grid=(n_tok // TM, D_MODEL // TN_PROJ),
        in_specs=in_specs,
        out_specs=pl.BlockSpec((TM, TN_PROJ), lambda i, j: (i + blk0, j)),
        out_shape=jax.ShapeDtypeStruct((N_TOK, D_MODEL), f32),
        input_output_aliases=aliases,
        compiler_params=_params(2),
        name="proj_res",
    )(*args)


def _win_kernel(x_ref, g_ref, sh_ref, sc_ref, w_ref, o_ref, ssq_ref, h_scr, *, tn):
    j = pl.program_id(1)
    nu = D_INNER // tn

    @pl.when(j == 0)
    def _():
        h_scr[...] = _adaln(x_ref[...], g_ref[...], sh_ref[...], sc_ref[...]).astype(bf16)

    y = jax.nn.gelu(jnp.dot(h_scr[...], w_ref[...], preferred_element_type=f32))
    o_ref[...] = y.astype(bf16)

    @pl.when(j == nu)
    def _():
        ssq_ref[...] = jnp.zeros_like(ssq_ref)

    @pl.when(j >= nu)
    def _():
        y2 = y * y
        part = y2[:, 0:LANES]
        for c in range(1, tn // LANES):
            part = part + y2[:, c * LANES:(c + 1) * LANES]
        ssq_ref[...] += part


def _chunk_in(x, mod, layer, gain, w_in):
    tn = TN_QKV
    return pl.pallas_call(
        functools.partial(_win_kernel, tn=tn),
        grid=(N_TOK // TM, 2 * D_INNER // tn),
        in_specs=[
            pl.BlockSpec((TM, D_MODEL), lambda i, j: (i, 0)),
            pl.BlockSpec((1, D_MODEL), lambda i, j: (0, 0)),
            _mod_spec(layer, 0, TM),
            _mod_spec(layer, 1, TM),
            pl.BlockSpec((D_MODEL, tn), lambda i, j: (0, j)),
        ],
        out_specs=[
            pl.BlockSpec((TM, tn), lambda i, j: (i, j)),
            pl.BlockSpec((TM, LANES), lambda i, j: (i, 0)),
        ],
        out_shape=[jax.ShapeDtypeStruct((N_TOK, 2 * D_INNER), bf16), jax.ShapeDtypeStruct((N_TOK, LANES), f32)],
        scratch_shapes=[pltpu.VMEM((TM, D_MODEL), bf16)],
        compiler_params=_params(2),
        name="chunk_in",
    )(x, gain, mod, mod, w_in)


def _sgu_out_kernel(u_ref, v_ref, ssq_ref, vg_ref, ws_ref, bs_ref, w_ref, x_ref, gate_ref, o_ref, a_scr):
    j = pl.program_id(1)

    @pl.when(j == 0)
    def _():
        rstd = lax.rsqrt(jnp.sum(ssq_ref[...], axis=-1, keepdims=True) * (1.0 / D_INNER) + RMS_EPS)
        for c in range(TM // CHUNK):
            rows = slice(c * CHUNK, (c + 1) * CHUNK)
            rs = rstd[rows]
            for g in range(N_SGU_GROUPS):
                cols = slice(g * SGU_GROUP_DIM, (g + 1) * SGU_GROUP_DIM)
                vn = (v_ref[rows, cols].astype(f32) * rs * vg_ref[:, cols]).astype(bf16)
                vm = jnp.dot(ws_ref[g].astype(bf16), vn, preferred_element_type=f32) + bs_ref[g]
                a_scr[rows, cols] = (u_ref[rows, cols].astype(f32) * vm).astype(bf16)

    o_ref[...] = x_ref[...] + gate_ref[...] * jnp.dot(a_scr[...], w_ref[...], preferred_element_type=f32)


def _chunk_out(uv, ssq, v_gain, w_s, b_s, w_out, x, mod, layer):
    return pl.pallas_call(
        _sgu_out_kernel,
        grid=(N_TOK // TM, D_MODEL // TN_PROJ),
        in_specs=[
            pl.BlockSpec((TM, D_INNER), lambda i, j: (i, 0)),
            pl.BlockSpec((TM, D_INNER), lambda i, j: (i, 1)),
            pl.BlockSpec((TM, LANES), lambda i, j: (i, 0)),
            pl.BlockSpec((1, D_INNER), lambda i, j: (0, 0)),
            pl.BlockSpec((N_SGU_GROUPS, CHUNK, CHUNK), lambda i, j: (0, 0, 0)),
            pl.BlockSpec((N_SGU_GROUPS, CHUNK, 1), lambda i, j: (0, 0, 0)),
            pl.BlockSpec((D_INNER, TN_PROJ), lambda i, j: (0, j)),
            pl.BlockSpec((TM, TN_PROJ), lambda i, j: (i, j)),
            _mod_spec(layer, 2, TM, tn=TN_PROJ, col=True),
        ],
        out_specs=pl.BlockSpec((TM, TN_PROJ), lambda i, j: (i, j)),
        out_shape=jax.ShapeDtypeStruct((N_TOK, D_MODEL), f32),
        scratch_shapes=[pltpu.VMEM((TM, D_INNER), bf16)],
        compiler_params=_params(2),
        name="sgu_out",
    )(uv, uv, ssq, v_gain, w_s, b_s, w_out, x, mod)


def _router_kernel(x_ref, g_ref, sh_ref, sc_ref, whi_ref, wlo_ref, b_ref, hp_ref, r_ref):
    h = _adaln(x_ref[...], g_ref[...], sh_ref[...], sc_ref[...])
    hb = h.astype(bf16)
    hf = hb.astype(f32)
    lo = (h - hf).astype(bf16)
    logits = (jnp.dot(hb, whi_ref[...], preferred_element_type=f32)
              + (jnp.dot(lo, whi_ref[...], preferred_element_type=f32)
                 + jnp.dot(hb, wlo_ref[...], preferred_element_type=f32))) + b_ref[...]

    bits = lax.bitcast_convert_type(hf, u32)
    hp_ref[...] = (bits[:, :HALF_D] >> 16) | bits[:, HALF_D:]

    lane = lax.broadcasted_iota(i32, logits.shape, 1).astype(f32)
    neg = -jnp.inf
    big = float(ROUTER_LANES)
    lg = jnp.where(lane < N_EXPERT_GROUPS, logits, neg)
    mg = jnp.max(lg, axis=-1, keepdims=True)
    pg_top = 1.0 / jnp.sum(jnp.exp(lg - mg), axis=-1, keepdims=True)
    grp = jnp.min(jnp.where(lg == mg, lane, big), axis=-1, keepdims=True)
    lane0 = EXPERT_LANE0 + EXPERTS_PER_GROUP * grp
    le = jnp.where((lane >= lane0) & (lane < lane0 + EXPERTS_PER_GROUP), logits, neg)
    m1 = jnp.max(le, axis=-1, keepdims=True)
    i1 = jnp.min(jnp.where(le == m1, lane, big), axis=-1, keepdims=True)
    le2 = jnp.where(lane == i1, neg, le)
    m2 = jnp.max(le2, axis=-1, keepdims=True)
    i2 = jnp.min(jnp.where(le2 == m2, lane, big), axis=-1, keepdims=True)
    e2 = jnp.exp(m2 - m1)
    p1 = 1.0 / (1.0 + e2)
    p2 = e2 * p1
    r_ref[...] = jnp.where(lane == 0, i1 - EXPERT_LANE0,
                           jnp.where(lane == 1, i2 - EXPERT_LANE0,
                                     jnp.where(lane == 2, pg_top * p1,
                                               jnp.where(lane == 3, pg_top * p2, 0.0))))


def _router(x, mod, layer, gain, w_group, b_group, w_expert, b_expert):
    tm = 256
    pad0 = jnp.zeros((D_MODEL, EXPERT_LANE0 - N_EXPERT_GROUPS), f32)
    pad1 = jnp.zeros((D_MODEL, ROUTER_LANES - EXPERT_LANE0 - N_EXPERTS), f32)
    w_r = jnp.concatenate([w_group, pad0, w_expert, pad1], axis=1)
    w_hi = w_r.astype(bf16)
    w_lo = (w_r - w_hi.astype(f32)).astype(bf16)
    b_r = jnp.concatenate([b_group, pad0[0], b_expert, pad1[0]])[None, :]
    return pl.pallas_call(
        _router_kernel,
        grid=(N_TOK // tm, 1),
        in_specs=[
            pl.BlockSpec((tm, D_MODEL), lambda i, j: (i, 0)),
            pl.BlockSpec((1, D_MODEL), lambda i, j: (0, 0)),
            _mod_spec(layer, 3, tm),
            _mod_spec(layer, 4, tm),
            pl.BlockSpec((D_MODEL, ROUTER_LANES), lambda i, j: (0, 0)),
            pl.BlockSpec((D_MODEL, ROUTER_LANES), lambda i, j: (0, 0)),
            pl.BlockSpec((1, ROUTER_LANES), lambda i, j: (0, 0)),
        ],
        out_specs=[
            pl.BlockSpec((tm, HALF_D), lambda i, j: (i, 0)),
            pl.BlockSpec((tm, ROUTER_LANES), lambda i, j: (i, 0)),
        ],
        out_shape=[jax.ShapeDtypeStruct((N_TOK, HALF_D), u32), jax.ShapeDtypeStruct((N_TOK, ROUTER_LANES), f32)],
        compiler_params=_params(2),
        name="router",
    )(x, gain, mod, mod, w_hi, w_lo, b_r)


def _moe_layout(route):
    flat_e = route[:, 0:TOP_K].astype(i32).reshape(-1)
    onehot = (flat_e[:, None] == jnp.arange(N_EXPERTS, dtype=i32)[None, :]).astype(i32)
    counts = jnp.sum(onehot, axis=0)
    pcounts = ((counts + TE - 1) // TE) * TE
    pend = jnp.cumsum(pcounts)
    pstart = pend - pcounts
    n_used = pend[-1] // TE
    tiles = jnp.arange(N_ETILES, dtype=i32)
    tile_e = jnp.sum((tiles[:, None] * TE >= pend[None, :]).astype(i32), axis=1)
    last_e = jnp.max(jnp.where(tiles < n_used, tile_e, 0))
    tile_e = jnp.minimum(tile_e, last_e)
    tile_blk = jnp.minimum(tiles, n_used - 1)
    last_tile_row = jnp.where(counts > 0, pend - TE, -1)
    rank = jnp.sum(onehot * (jnp.cumsum(onehot, axis=0) - 1), axis=1)
    pos = jnp.sum(onehot * pstart[None, :], axis=1) + rank
    return tile_e, tile_blk, n_used.reshape(1).astype(i32), last_tile_row.astype(i32), pos.astype(i32)


def _scatter_kernel(pos_ref, ltr_ref, hp_ref, xs_hbm, zero_scr, sem, zsem):
    i = pl.program_id(0)

    @pl.when(i == 0)
    def _():
        zero_scr[...] = jnp.zeros_like(zero_scr)

        def zero_copy(e):
            row0 = pl.multiple_of(ltr_ref[e], TE)
            return pltpu.make_async_copy(zero_scr, xs_hbm.at[pl.ds(row0, TE)], zsem)

        for e in range(N_EXPERTS):
            @pl.when(ltr_ref[e] >= 0)
            def _():
                zero_copy(e).start()
        for e in range(N_EXPERTS):
            @pl.when(ltr_ref[e] >= 0)
            def _():
                zero_copy(e).wait()

    base = i * (TT * TOP_K)

    def issue(r, carry):
        for k in range(TOP_K):
            p = pos_ref[base + TOP_K * r + k]
            pltpu.make_async_copy(hp_ref.at[pl.ds(r, 1)], xs_hbm.at[pl.ds(p, 1)], sem).start()
        return carry

    lax.fori_loop(0, TT, issue, 0)
    for k in range(TOP_K):
        pltpu.make_async_copy(hp_ref, xs_hbm.at[pl.ds(0, TT)], sem).wait()


def _dispatch(hp, pos, last_tile_row):
    grid_spec = pltpu.PrefetchScalarGridSpec(
        num_scalar_prefetch=2,
        grid=(N_TOK // TT,),
        in_specs=[pl.BlockSpec((TT, HALF_D), lambda i, pos, ltr: (i, 0))],
        out_specs=pl.BlockSpec(memory_space=pl.ANY),
        scratch_shapes=[pltpu.VMEM((TE, HALF_D), u32), pltpu.SemaphoreType.DMA(()), pltpu.SemaphoreType.DMA(())],
    )
    return pl.pallas_call(
        _scatter_kernel,
        grid_spec=grid_spec,
        out_shape=jax.ShapeDtypeStruct((R_MAX, HALF_D), u32),
        compiler_params=_params(1),
        name="dispatch",
    )(pos, last_tile_row, hp)


def _gate_up_kernel(te_ref, tb_ref, nu_ref, xs_ref, wg_ref, wu_ref, o_ref, wb_scr):
    t = pl.program_id(1)
    new_expert = (t == 0) | (te_ref[t] != te_ref[jnp.maximum(t - 1, 0)])

    @pl.when(new_expert)
    def _():
        wb_scr[:, 0:TN_GU] = wg_ref[...].astype(bf16)
        wb_scr[:, TN_GU:2 * TN_GU] = wu_ref[...].astype(bf16)

    @pl.when(t < nu_ref[0])
    def _():
        w = xs_ref[...]
        lo = lax.bitcast_convert_type(w << 16, f32).astype(bf16)
        hi = lax.bitcast_convert_type(w & jnp.uint32(0xFFFF0000), f32).astype(bf16)
        acc = (jnp.dot(lo, wb_scr[0:HALF_D, :], preferred_element_type=f32)
               + jnp.dot(hi, wb_scr[HALF_D:D_MODEL, :], preferred_element_type=f32))
        o_ref[...] = (jax.nn.silu(acc[:, 0:TN_GU]) * acc[:, TN_GU:2 * TN_GU]).astype(bf16)


def _gate_up(xs, w_gate_up, layer, tile_e, tile_blk, n_used):
    nj = D_EXPERT // TN_GU
    grid_spec = pltpu.PrefetchScalarGridSpec(
        num_scalar_prefetch=3,
        grid=(nj, N_ETILES),
        in_specs=[
            pl.BlockSpec((TE, HALF_D), lambda j, t, te, tb, nu: (tb[t], 0)),
            pl.BlockSpec((None, None, D_MODEL, TN_GU), lambda j, t, te, tb, nu: (layer, te[t], 0, j)),
            pl.BlockSpec((None, None, D_MODEL, TN_GU), lambda j, t, te, tb, nu: (layer, te[t], 0, nj + j)),
        ],
        out_specs=pl.BlockSpec((TE, TN_GU), lambda j, t, te, tb, nu: (tb[t], j)),
        scratch_shapes=[pltpu.VMEM((D_MODEL, 2 * TN_GU), bf16)],
    )
    return pl.pallas_call(
        _gate_up_kernel,
        grid_spec=grid_spec,
        out_shape=jax.ShapeDtypeStruct((R_MAX, D_EXPERT), bf16),
        compiler_params=_params(2),
        name="moe_gate_up",
    )(tile_e, tile_blk, n_used, xs, w_gate_up, w_gate_up)


def _down_kernel(te_ref, tb_ref, nu_ref, h_ref, w_ref, o_ref, wb_scr):
    t = pl.program_id(1)
    new_expert = (t == 0) | (te_ref[t] != te_ref[jnp.maximum(t - 1, 0)])

    @pl.when(new_expert)
    def _():
        wb_scr[...] = w_ref[...].astype(bf16)

    @pl.when(t < nu_ref[0])
    def _():
        o_ref[...] = jnp.dot(h_ref[...], wb_scr[...], preferred_element_type=f32)


def _down(h1, w_down, layer, tile_e, tile_blk, n_used):
    grid_spec = pltpu.PrefetchScalarGridSpec(
        num_scalar_prefetch=3,
        grid=(D_MODEL // TN_DOWN, N_ETILES),
        in_specs=[
            pl.BlockSpec((TE, D_EXPERT), lambda j, t, te, tb, nu: (tb[t], 0)),
            pl.BlockSpec((None, None, D_EXPERT, TN_DOWN), lambda j, t, te, tb, nu: (layer, te[t], 0, j)),
        ],
        out_specs=pl.BlockSpec((TE, TN_DOWN), lambda j, t, te, tb, nu: (tb[t], j)),
        scratch_shapes=[pltpu.VMEM((D_EXPERT, TN_DOWN), bf16)],
    )
    return pl.pallas_call(
        _down_kernel,
        grid_spec=grid_spec,
        out_shape=jax.ShapeDtypeStruct((R_MAX, D_MODEL), f32),
        compiler_params=_params(2),
        name="moe_down",
    )(tile_e, tile_blk, n_used, h1, w_down)


def _combine_kernel(pos_ref, x_ref, gate_ref, r_ref, ys_hbm, o_ref, buf, sem, *, tok0):
    base = (pl.program_id(0) * TC + tok0) * TOP_K

    def issue(r, carry):
        for k in range(TOP_K):
            p = pos_ref[base + TOP_K * r + k]
            pltpu.make_async_copy(ys_hbm.at[pl.ds(p, 1)], buf.at[k, pl.ds(r, 1)], sem).start()
        return carry

    lax.fori_loop(0, TC, issue, 0)
    for k in range(TOP_K):
        pltpu.make_async_copy(ys_hbm.at[pl.ds(0, TC)], buf.at[k], sem).wait()
    r = r_ref[...]
    f = r[:, 2:3] * buf[0] + r[:, 3:4] * buf[1]
    o_ref[...] = x_ref[...] + gate_ref[...] * f


def _combine(x, mod, layer, route, ys, pos, *, tok0, n_tok):
    blk0 = tok0 // TC
    grid_spec = pltpu.PrefetchScalarGridSpec(
        num_scalar_prefetch=1,
        grid=(n_tok // TC,),
        in_specs=[
            pl.BlockSpec((TC, D_MODEL), lambda i, pos: (i + blk0, 0)),
            pl.BlockSpec((None, None, None, 1, D_MODEL), lambda i, pos: (layer, _mod_row(i + blk0, TC), 5, 0, 0)),
            pl.BlockSpec((TC, ROUTER_LANES), lambda i, pos: (i + blk0, 0)),
            pl.BlockSpec(memory_space=pl.ANY),
        ],
        out_specs=pl.BlockSpec((TC, D_MODEL), lambda i, pos: (i, 0)),
        scratch_shapes=[pltpu.VMEM((TOP_K, TC, D_MODEL), f32), pltpu.SemaphoreType.DMA(())],
    )
    return pl.pallas_call(
        functools.partial(_combine_kernel, tok0=tok0),
        grid_spec=grid_spec,
        out_shape=jax.ShapeDtypeStruct((n_tok, D_MODEL), f32),
        compiler_params=_params(1),
        name="combine",
    )(pos, x, mod, route, ys)


def _hier_moe(x, mod, layer, gain, w_group, b_group, w_expert, b_expert, w_gate_up, w_down, *, split):
    hp, route = _router(x, mod, layer, gain, w_group, b_group, w_expert, b_expert)
    tile_e, tile_blk, n_used, last_tile_row, pos = _moe_layout(route)
    xs = _dispatch(hp, pos, last_tile_row)
    h1 = _gate_up(xs, w_gate_up, layer, tile_e, tile_blk, n_used)
    ys = _down(h1, w_down, layer, tile_e, tile_blk, n_used)
    if split:
        return (_combine(x, mod, layer, route, ys, pos, tok0=0, n_tok=N_CTX),
                _combine(x, mod, layer, route, ys, pos, tok0=N_CTX, n_tok=N_LAT))
    return _combine(x, mod, layer, route, ys, pos, tok0=0, n_tok=N_TOK)


def kernel(x_prompt, x_sample, cache_k, cache_v, c, c_ctx, norm_gain, w_mod, b_mod, attn_w_qkv, attn_q_gain, attn_k_gain, attn_w_o, cm_w_in, cm_v_gain, cm_w_s, cm_b_s, cm_w_out, moe_w_group, moe_b_group, moe_w_expert, moe_b_expert, moe_w_gate_up, moe_w_down):
    x_ctx = x_prompt.reshape(N_CTX, D_MODEL)
    x_lat = x_sample.reshape(N_LAT, D_MODEL)
    cond =jnp.concatenate([c_ctx[None, :], c, jnp.zeros((MOD_ROWS - 1 - DEC_BATCH, D_MODEL), f32)], axis=0)
    mod = _modulation(cond, w_mod, b_mod)

    w_qkv = attn_w_qkv[0].astype(bf16)
    q_gain = attn_q_gain[0][None, :]
    k_gain = attn_k_gain[0][None, :]
    gain0 = norm_gain[0, 0][None, :]
    qkv_ctx, kv_new = _qkv_proj(x_ctx, mod, 0, gain0, w_qkv, q_gain, k_gain, tok0=0, with_cache=True)
    (qkv_lat,) = _qkv_proj(x_lat, mod, 0, gain0, w_qkv, q_gain, k_gain, tok0=N_CTX, rope_tabs=_rope_tables())
    o_ctx = _attention(qkv_ctx, n_batch=BATCH, seq=SEQ, tq=SEQ)
    ck = cache_k[:, 0].reshape(DEC_BATCH, PAST_LEN, KV_DIM)
    cv = cache_v[:, 0].reshape(DEC_BATCH, PAST_LEN, KV_DIM)
    o_lat = _attention(qkv_lat, n_batch=DEC_BATCH, seq=DEC_SEQ, tq=TQ, cache=(ck, cv))
    w_o = attn_w_o[0].astype(bf16)
    x = _proj_res(o_ctx, w_o, x_ctx, mod, 0, tok0=0)
    x = _proj_res(o_lat, w_o, x_lat, mod, 0, tok0=N_CTX, out=x)
    x = _hier_moe(x, mod, 0, norm_gain[0, 1][None, :], moe_w_group[0], moe_b_group[0], moe_w_expert[0],
                  moe_b_expert[0], moe_w_gate_up, moe_w_down, split=False)

    uv, ssq = _chunk_in(x, mod, 1, norm_gain[1, 0][None, :], cm_w_in[0].astype(bf16))
    x = _chunk_out(uv, ssq, cm_v_gain[0][None, :], cm_w_s[0], cm_b_s[0][:, :, None], cm_w_out[0].astype(bf16),
                   x, mod, 1)
    y_ctx, y_lat = _hier_moe(x, mod, 1, norm_gain[1, 1][None, :], moe_w_group[1], moe_b_group[1], moe_w_expert[1],
                             moe_b_expert[1], moe_w_gate_up, moe_w_down, split=True)

    new_k = kv_new[:, :KV_DIM].reshape(BATCH, 1, SEQ, N_KV_HEADS, HEAD_DIM)
    new_v = kv_new[:, KV_DIM:].reshape(BATCH, 1, SEQ, N_KV_HEADS, HEAD_DIM)
    return (y_ctx.reshape(BATCH, SEQ, D_MODEL), y_lat.reshape(DEC_BATCH, DEC_SEQ, D_MODEL), new_k, new_v)
```

```python
import functools

import jax
import jax.numpy as jnp
from jax import lax
from jax.experimental import pallas as pl
from jax.experimental.pallas import tpu as pltpu

f32 = jnp.float32
bf16 = jnp.bfloat16
i32 = jnp.int32
u32 = jnp.uint32

D_MODEL = 4096
BATCH, SEQ = 16, 256
DEC_BATCH, DEC_SEQ = 4, 2048
PAST_LEN = 256
DEPTH = 2
GRID_W = 64
N_HEADS, N_KV_HEADS, HEAD_DIM = 32, 8, 128
GQA = N_HEADS // N_KV_HEADS
Q_DIM = N_HEADS * HEAD_DIM
KV_DIM = N_KV_HEADS * HEAD_DIM
QKV_DIM = Q_DIM + 2 * KV_DIM
AXIS_DIM = HEAD_DIM // 2
ROPE_THETA = 10000.0
CHUNK = 128
D_INNER = D_MODEL
N_SGU_GROUPS = 8
SGU_GROUP_DIM = D_INNER // N_SGU_GROUPS
N_EXPERT_GROUPS, EXPERTS_PER_GROUP = 4, 8
N_EXPERTS = N_EXPERT_GROUPS * EXPERTS_PER_GROUP
TOP_K = 2
D_EXPERT = D_MODEL // 4
N_MOD = 6
RMS_EPS = 1e-6

N_CTX = BATCH * SEQ
N_LAT = DEC_BATCH * DEC_SEQ
N_TOK = N_CTX + N_LAT
N_COPIES = N_TOK * TOP_K
MOD_ROWS = 8

LANES = 128
HALF_D = D_MODEL // 2

TM = 512
TN_QKV = 1024
TN_PROJ = 1024
TN_MOD = 512
TQ = 256
KEY_CHUNK = 128
TE = 256
N_ETILES = N_COPIES // TE + N_EXPERTS
R_MAX = N_ETILES * TE
TT = 512
TC = 256
LOG2E = 1.4426950408889634
ROUTER_LANES = 128
EXPERT_LANE0 = 32

VMEM_LIMIT = 56 * 1024 * 1024


def _params(n_axes):
    return pltpu.CompilerParams(dimension_semantics=("arbitrary",) * n_axes, vmem_limit_bytes=VMEM_LIMIT)


def _mod_row(i, tm):
    start = i * tm
    return jnp.where(start < N_CTX, 0, 1 + (start - N_CTX) // DEC_SEQ)


def _mod_spec(layer, k, tm, tn=D_MODEL, col=None, row_fn=None):
    row_fn = row_fn or (lambda i: _mod_row(i, tm))
    if col is None:
        return pl.BlockSpec((None, None, None, 1, tn), lambda i, j: (layer, row_fn(i), k, 0, 0))
    return pl.BlockSpec((None, None, None, 1, tn), lambda i, j: (layer, row_fn(i), k, 0, j))


def _adaln(x, gain, shift, scale):
    ms = jnp.mean(x * x, axis=-1, keepdims=True)
    return x * lax.rsqrt(ms + RMS_EPS) * gain * (1.0 + scale) + shift


def _mod_kernel(c_ref, w_ref, b_ref, o_ref):
    s = jax.nn.silu(c_ref[...]).astype(bf16)
    o_ref[...] = jnp.dot(s, w_ref[...].astype(bf16), preferred_element_type=f32) + b_ref[...]


def _modulation(cond, w_mod, b_mod):
    n = N_MOD * D_MODEL
    out = pl.pallas_call(
        _mod_kernel,
        grid=(DEPTH, n // TN_MOD),
        in_specs=[
            pl.BlockSpec((MOD_ROWS, D_MODEL), lambda l, j: (0, 0)),
            pl.BlockSpec((None, D_MODEL, TN_MOD), lambda l, j: (l, 0, j)),
            pl.BlockSpec((None, 1, TN_MOD), lambda l, j: (l, 0, j)),
        ],
        out_specs=pl.BlockSpec((None, MOD_ROWS, TN_MOD), lambda l, j: (l, 0, j)),
        out_shape=jax.ShapeDtypeStruct((DEPTH, MOD_ROWS, n), f32),
        compiler_params=_params(2),
        name="modulation",
    )(cond, w_mod, b_mod.reshape(DEPTH, 1, n))
    return out.reshape(DEPTH, MOD_ROWS, N_MOD, 1, D_MODEL)


def _qkv_kernel(*refs, rope, with_cache):
    x_ref, g_ref, sh_ref, sc_ref, w_ref, qg_ref, kg_ref = refs[:7]
    pos = 7
    if rope:
        cos_ref, sa_ref, sb_ref = refs[pos:pos + 3]
        pos += 3
    o_ref = refs[pos]
    pos += 1
    if with_cache:
        kv_ref = refs[pos]
        pos += 1
    h_scr = refs[pos]

    j = pl.program_id(1)
    nq = Q_DIM // TN_QKV
    nk = KV_DIM // TN_QKV

    @pl.when(j == 0)
    def _():
        h_scr[...] = _adaln(x_ref[...], g_ref[...], sh_ref[...], sc_ref[...]).astype(bf16)

    acc = jnp.dot(h_scr[...], w_ref[...], preferred_element_type=f32)

    def normed_head(hh, gain):
        a = acc[:, hh * HEAD_DIM:(hh + 1) * HEAD_DIM]
        y = a * lax.rsqrt(jnp.mean(a * a, axis=-1, keepdims=True) + RMS_EPS) * gain
        if rope:
            y = (y * cos_ref[...] + pltpu.roll(y, HEAD_DIM - AXIS_DIM // 2, 1) * sa_ref[...]
                 + pltpu.roll(y, AXIS_DIM // 2, 1) * sb_ref[...])
        return y

    @pl.when(j < nq)
    def _():
        gain = qg_ref[...] * (HEAD_DIM ** -0.5 * LOG2E)
        for hh in range(TN_QKV // HEAD_DIM):
            o_ref[:, hh * HEAD_DIM:(hh + 1) * HEAD_DIM] = normed_head(hh, gain).astype(bf16)

    @pl.when((j >= nq) & (j < nq + nk))
    def _():
        for hh in range(TN_QKV // HEAD_DIM):
            y = normed_head(hh, kg_ref[...])
            o_ref[:, hh * HEAD_DIM:(hh + 1) * HEAD_DIM] = y.astype(bf16)
            if with_cache:
                kv_ref[:, hh * HEAD_DIM:(hh + 1) * HEAD_DIM] = y

    @pl.when(j >= nq + nk)
    def _():
        o_ref[...] = acc.astype(bf16)
        if with_cache:
            kv_ref[...] = acc


def _qkv_proj(x, mod, layer, gain, w_qkv, q_gain, k_gain, *, tok0, rope_tabs=None, with_cache=False):
    n_tok = x.shape[0]
    blk0 = tok0 // TM
    nj = QKV_DIM // TN_QKV
    nq = Q_DIM // TN_QKV
    row_fn = lambda i: _mod_row(i + blk0, TM)
    in_specs = [
        pl.BlockSpec((TM, D_MODEL), lambda i, j: (i, 0)),
        pl.BlockSpec((1, D_MODEL), lambda i, j: (0, 0)),
        _mod_spec(layer, 0, TM, row_fn=row_fn),
        _mod_spec(layer, 1, TM, row_fn=row_fn),
        pl.BlockSpec((D_MODEL, TN_QKV), lambda i, j: (0, j)),
        pl.BlockSpec((1, HEAD_DIM), lambda i, j: (0, 0)),
        pl.BlockSpec((1, HEAD_DIM), lambda i, j: (0, 0)),
    ]
    args = [x, gain, mod, mod, w_qkv, q_gain, k_gain]
    if rope_tabs is not None:
        nb = DEC_SEQ // TM
        in_specs += [pl.BlockSpec((TM, HEAD_DIM), lambda i, j: (i % nb, 0))] * 3
        args += list(rope_tabs)
    out_specs = [pl.BlockSpec((TM, TN_QKV), lambda i, j: (i, j))]
    out_shape = [jax.ShapeDtypeStruct((n_tok, QKV_DIM), bf16)]
    if with_cache:
        out_specs.append(pl.BlockSpec((TM, TN_QKV), lambda i, j: (i, jnp.maximum(j - nq, 0))))
        out_shape.append(jax.ShapeDtypeStruct((n_tok, 2 * KV_DIM), f32))
    return pl.pallas_call(
        functools.partial(_qkv_kernel, rope=rope_tabs is not None, with_cache=with_cache),
        grid=(n_tok // TM, nj),
        in_specs=in_specs,
        out_specs=out_specs,
        out_shape=out_shape,
        scratch_shapes=[pltpu.VMEM((TM, D_MODEL), bf16)],
        compiler_params=_params(2),
        name="qkv_rope" if rope_tabs is not None else "qkv_ctx",
    )(*args)


def _rope_tables():
    t = jnp.arange(DEC_SEQ)
    row = (t // GRID_W).astype(f32)
    col = (t % GRID_W).astype(f32)
    inv_freq = ROPE_THETA ** (-jnp.arange(0, AXIS_DIM, 2, dtype=f32) / AXIS_DIM)
    lane = jnp.arange(HEAD_DIM)
    freq = inv_freq[lane % (AXIS_DIM // 2)]
    pos = jnp.where(lane[None, :] < AXIS_DIM, row[:, None], col[:, None])
    ang = pos * freq[None, :]
    first = (lane % AXIS_DIM) < (AXIS_DIM // 2)
    sin = jnp.sin(ang)
    return jnp.cos(ang), jnp.where(first[None, :], -sin, 0.0), jnp.where(first[None, :], 0.0, sin)


def _attn_kernel(*refs, with_cache):
    q_ref, k_ref, v_ref = refs[:3]
    if with_cache:
        ck_ref, cv_ref, o_ref, k_all, vt_all = refs[3:8]
    else:
        o_ref, k_all, vt_all = refs[3:6]
    n_keys = k_all.shape[0]
    kc = KEY_CHUNK

    @pl.when(pl.program_id(2) == 0)
    def _():
        off = 0
        if with_cache:
            k_all[0:PAST_LEN, :] = ck_ref[...].astype(bf16)
            for c in range(PAST_LEN // kc):
                vt_all[:, c * kc:(c + 1) * kc] = cv_ref[c * kc:(c + 1) * kc, :].T.astype(bf16)
            off = PAST_LEN
        k_all[off:n_keys, :] = k_ref[...]
        for c in range((n_keys - off) // kc):
            vt_all[:, off + c * kc:off + (c + 1) * kc] = v_ref[c * kc:(c + 1) * kc, :].astype(f32).T.astype(bf16)

    nt_dims = (((1,), (1,)), ((), ()))
    for g in range(GQA):
        qg = q_ref[:, g * HEAD_DIM:(g + 1) * HEAD_DIM]
        m = l = acc = None
        for c in range(n_keys // kc):
            st = lax.dot_general(k_all[c * kc:(c + 1) * kc, :], qg, nt_dims, preferred_element_type=f32)
            ms = jnp.max(st, axis=0, keepdims=True)
            vt = vt_all[:, c * kc:(c + 1) * kc]
            if m is None:
                m = ms
                p = jnp.exp2(st - m)
                l = jnp.sum(p, axis=0, keepdims=True)
                acc = jnp.dot(vt, p.astype(bf16), preferred_element_type=f32)
            else:
                m_new = jnp.maximum(m, ms)
                alpha = jnp.exp2(m - m_new)
                p = jnp.exp2(st - m_new)
                l = alpha * l + jnp.sum(p, axis=0, keepdims=True)
                acc = alpha * acc + jnp.dot(vt, p.astype(bf16), preferred_element_type=f32)
                m = m_new
        o_ref[:, g * HEAD_DIM:(g + 1) * HEAD_DIM] = (acc / l).T.astype(bf16)


def _attention(qkv, *, n_batch, seq, tq, cache=None):
    nqb = seq // tq
    n_keys = seq + (PAST_LEN if cache is not None else 0)
    kcol = Q_DIM // HEAD_DIM
    vcol = (Q_DIM + KV_DIM) // HEAD_DIM
    in_specs = [
        pl.BlockSpec((tq, GQA * HEAD_DIM), lambda b, h, i: (b * nqb + i, h)),
        pl.BlockSpec((seq, HEAD_DIM), lambda b, h, i: (b, kcol + h)),
        pl.BlockSpec((seq, HEAD_DIM), lambda b, h, i: (b, vcol + h)),
    ]
    args = [qkv, qkv, qkv]
    if cache is not None:
        in_specs += [pl.BlockSpec((None, PAST_LEN, HEAD_DIM), lambda b, h, i: (b, 0, h))] * 2
        args += list(cache)
    return pl.pallas_call(
        functools.partial(_attn_kernel, with_cache=cache is not None),
        grid=(n_batch, N_KV_HEADS, nqb),
        in_specs=in_specs,
        out_specs=pl.BlockSpec((tq, GQA * HEAD_DIM), lambda b, h, i: (b * nqb + i, h)),
        out_shape=jax.ShapeDtypeStruct((n_batch * seq, Q_DIM), bf16),
        scratch_shapes=[pltpu.VMEM((n_keys, HEAD_DIM), bf16), pltpu.VMEM((HEAD_DIM, n_keys), bf16)],
        compiler_params=_params(3),
        name="attn_lat" if cache is not None else "attn_ctx",
    )(*args)


def _proj_res_kernel(a_ref, w_ref, x_ref, gate_ref, *rest):
    o_ref = rest[-1]
    o_ref[...] = x_ref[...] + gate_ref[...] * jnp.dot(a_ref[...], w_ref[...], preferred_element_type=f32)


def _proj_res(a, w, x, mod, layer, *, tok0, out=None):
    n_tok = a.shape[0]
    blk0 = tok0 // TM
    row_fn = lambda i: _mod_row(i + blk0, TM)
    in_specs = [
        pl.BlockSpec((TM, a.shape[1]), lambda i, j: (i, 0)),
        pl.BlockSpec((a.shape[1], TN_PROJ), lambda i, j: (0, j)),
        pl.BlockSpec((TM, TN_PROJ), lambda i, j: (i, j)),
        _mod_spec(layer, 2, TM, tn=TN_PROJ, col=True, row_fn=row_fn),
    ]
    args = [a, w, x, mod]
    aliases = {}
    if out is not None:
        in_specs.append(pl.BlockSpec(memory_space=pl.ANY))
        args.append(out)
        aliases = {len(args) - 1: 0}
    return pl.pallas_call(
        _proj_res_kernel,
        grid=(n_tok // TM, D_MODEL // TN_PROJ),
        in_specs=in_specs,
        out_specs=pl.BlockSpec((TM, TN_PROJ), lambda i, j: (i + blk0, j)),
        out_shape=jax.ShapeDtypeStruct((N_TOK, D_MODEL), f32),
        input_output_aliases=aliases,
        compiler_params=_params(2),
        name="proj_res",
    )(*args)


def _win_kernel(x_ref, g_ref, sh_ref, sc_ref, w_ref, o_ref, ssq_ref, h_scr, *, tn):
    j = pl.program_id(1)
    nu = D_INNER // tn

    @pl.when(j == 0)
    def _():
        h_scr[...] = _adaln(x_ref[...], g_ref[...], sh_ref[...], sc_ref[...]).astype(bf16)

    y = jax.nn.gelu(jnp.dot(h_scr[...], w_ref[...], preferred_element_type=f32))
    o_ref[...] = y.astype(bf16)

    @pl.when(j == nu)
    def _():
        ssq_ref[...] = jnp.zeros_like(ssq_ref)

    @pl.when(j >= nu)
    def _():
        y2 = y * y
        part = y2[:, 0:LANES]
        for c in range(1, tn // LANES):
            part = part + y2[:, c * LANES:(c + 1) * LANES]
        ssq_ref[...] += part


def _chunk_in(x, mod, layer, gain, w_in):
    tn = TN_QKV
    return pl.pallas_call(
        functools.partial(_win_kernel, tn=tn),
        grid=(N_TOK // TM, 2 * D_INNER // tn),
        in_specs=[
            pl.BlockSpec((TM, D_MODEL), lambda i, j: (i, 0)),
            pl.BlockSpec((1, D_MODEL), lambda i, j: (0, 0)),
            _mod_spec(layer, 0, TM),
            _mod_spec(layer, 1, TM),
            pl.BlockSpec((D_MODEL, tn), lambda i, j: (0, j)),
        ],
        out_specs=[
            pl.BlockSpec((TM, tn), lambda i, j: (i, j)),
            pl.BlockSpec((TM, LANES), lambda i, j: (i, 0)),
        ],
        out_shape=[jax.ShapeDtypeStruct((N_TOK, 2 * D_INNER), bf16), jax.ShapeDtypeStruct((N_TOK, LANES), f32)],
        scratch_shapes=[pltpu.VMEM((TM, D_MODEL), bf16)],
        compiler_params=_params(2),
        name="chunk_in",
    )(x, gain, mod, mod, w_in)


def _sgu_out_kernel(u_ref, v_ref, ssq_ref, vg_ref, ws_ref, bs_ref, w_ref, x_ref, gate_ref, o_ref, a_scr):
    j = pl.program_id(1)

    @pl.when(j == 0)
    def _():
        rstd = lax.rsqrt(jnp.sum(ssq_ref[...], axis=-1, keepdims=True) * (1.0 / D_INNER) + RMS_EPS)
        for c in range(TM // CHUNK):
            rows = slice(c * CHUNK, (c + 1) * CHUNK)
            rs = rstd[rows]
            for g in range(N_SGU_GROUPS):
                cols = slice(g * SGU_GROUP_DIM, (g + 1) * SGU_GROUP_DIM)
                vn = (v_ref[rows, cols].astype(f32) * rs * vg_ref[:, cols]).astype(bf16)
                vm = jnp.dot(ws_ref[g].astype(bf16), vn, preferred_element_type=f32) + bs_ref[g]
                a_scr[rows, cols] = (u_ref[rows, cols].astype(f32) * vm).astype(bf16)

    o_ref[...] = x_ref[...] + gate_ref[...] * jnp.dot(a_scr[...], w_ref[...], preferred_element_type=f32)


def _chunk_out(uv, ssq, v_gain, w_s, b_s, w_out, x, mod, layer):
    return pl.pallas_call(
        _sgu_out_kernel,
        grid=(N_TOK // TM, D_MODEL // TN_PROJ),
        in_specs=[
            pl.BlockSpec((TM, D_INNER), lambda i, j: (i, 0)),
            pl.BlockSpec((TM, D_INNER), lambda i, j: (i, 1)),
            pl.BlockSpec((TM, LANES), lambda i, j: (i, 0)),
            pl.BlockSpec((1, D_INNER), lambda i, j: (0, 0)),
            pl.BlockSpec((N_SGU_GROUPS, CHUNK, CHUNK), lambda i, j: (0, 0, 0)),
            pl.BlockSpec((N_SGU_GROUPS, CHUNK, 1), lambda i, j: (0, 0, 0)),
            pl.BlockSpec((D_INNER, TN_PROJ), lambda i, j: (0, j)),
            pl.BlockSpec((TM, TN_PROJ), lambda i, j: (i, j)),
            _mod_spec(layer, 2, TM, tn=TN_PROJ, col=True),
        ],
        out_specs=pl.BlockSpec((TM, TN_PROJ), lambda i, j: (i, j)),
        out_shape=jax.ShapeDtypeStruct((N_TOK, D_MODEL), f32),
        scratch_shapes=[pltpu.VMEM((TM, D_INNER), bf16)],
        compiler_params=_params(2),
        name="sgu_out",
    )(uv, uv, ssq, v_gain, w_s, b_s, w_out, x, mod)


def _router_kernel(x_ref, g_ref, sh_ref, sc_ref, whi_ref, wlo_ref, b_ref, hp_ref, r_ref):
    h = _adaln(x_ref[...], g_ref[...], sh_ref[...], sc_ref[...])
    hb = h.astype(bf16)
    hf = hb.astype(f32)
    lo = (h - hf).astype(bf16)
    logits = (jnp.dot(hb, whi_ref[...], preferred_element_type=f32)
              + (jnp.dot(lo, whi_ref[...], preferred_element_type=f32)
                 + jnp.dot(hb, wlo_ref[...], preferred_element_type=f32))) + b_ref[...]

    bits = lax.bitcast_convert_type(hf, u32)
    hp_ref[...] = (bits[:, :HALF_D] >> 16) | bits[:, HALF_D:]

    lane = lax.broadcasted_iota(i32, logits.shape, 1).astype(f32)
    neg = -jnp.inf
    big = float(ROUTER_LANES)
    lg = jnp.where(lane < N_EXPERT_GROUPS, logits, neg)
    mg = jnp.max(lg, axis=-1, keepdims=True)
    pg_top = 1.0 / jnp.sum(jnp.exp(lg - mg), axis=-1, keepdims=True)
    grp = jnp.min(jnp.where(lg == mg, lane, big), axis=-1, keepdims=True)
    lane0 = EXPERT_LANE0 + EXPERTS_PER_GROUP * grp
    le = jnp.where((lane >= lane0) & (lane < lane0 + EXPERTS_PER_GROUP), logits, neg)
    m1 = jnp.max(le, axis=-1, keepdims=True)
    i1 = jnp.min(jnp.where(le == m1, lane, big), axis=-1, keepdims=True)
    le2 = jnp.where(lane == i1, neg, le)
    m2 = jnp.max(le2, axis=-1, keepdims=True)
    i2 = jnp.min(jnp.where(le2 == m2, lane, big), axis=-1, keepdims=True)
    e2 = jnp.exp(m2 - m1)
    p1 = 1.0 / (1.0 + e2)
    p2 = e2 * p1
    r_ref[...] = jnp.where(lane == 0, i1 - EXPERT_LANE0,
                           jnp.where(lane == 1, i2 - EXPERT_LANE0,
                                     jnp.where(lane == 2, pg_top * p1,
                                               jnp.where(lane == 3, pg_top * p2, 0.0))))


def _router(x, mod, layer, gain, w_group, b_group, w_expert, b_expert):
    tm = 256
    pad0 = jnp.zeros((D_MODEL, EXPERT_LANE0 - N_EXPERT_GROUPS), f32)
    pad1 = jnp.zeros((D_MODEL, ROUTER_LANES - EXPERT_LANE0 - N_EXPERTS), f32)
    w_r = jnp.concatenate([w_group, pad0, w_expert, pad1], axis=1)
    w_hi = w_r.astype(bf16)
    w_lo = (w_r - w_hi.astype(f32)).astype(bf16)
    b_r = jnp.concatenate([b_group, pad0[0], b_expert, pad1[0]])[None, :]
    return pl.pallas_call(
        _router_kernel,
        grid=(N_TOK // tm, 1),
        in_specs=[
            pl.BlockSpec((tm, D_MODEL), lambda i, j: (i, 0)),
            pl.BlockSpec((1, D_MODEL), lambda i, j: (0, 0)),
            _mod_spec(layer, 3, tm),
            _mod_spec(layer, 4, tm),
            pl.BlockSpec((D_MODEL, ROUTER_LANES), lambda i, j: (0, 0)),
            pl.BlockSpec((D_MODEL, ROUTER_LANES), lambda i, j: (0, 0)),
            pl.BlockSpec((1, ROUTER_LANES), lambda i, j: (0, 0)),
        ],
        out_specs=[
            pl.BlockSpec((tm, HALF_D), lambda i, j: (i, 0)),
            pl.BlockSpec((tm, ROUTER_LANES), lambda i, j: (i, 0)),
        ],
        out_shape=[jax.ShapeDtypeStruct((N_TOK, HALF_D), u32), jax.ShapeDtypeStruct((N_TOK, ROUTER_LANES), f32)],
        compiler_params=_params(2),
        name="router",
    )(x, gain, mod, mod, w_hi, w_lo, b_r)


def _moe_layout(route):
    flat_e = route[:, 0:TOP_K].astype(i32).reshape(-1)
    onehot = (flat_e[:, None] == jnp.arange(N_EXPERTS, dtype=i32)[None, :]).astype(i32)
    counts = jnp.sum(onehot, axis=0)
    pcounts = ((counts + TE - 1) // TE) * TE
    pend = jnp.cumsum(pcounts)
    pstart = pend - pcounts
    n_used = pend[-1] // TE
    tiles = jnp.arange(N_ETILES, dtype=i32)
    tile_e = jnp.sum((tiles[:, None] * TE >= pend[None, :]).astype(i32), axis=1)
    last_e = jnp.max(jnp.where(tiles < n_used, tile_e, 0))
    tile_e = jnp.minimum(tile_e, last_e)
    tile_blk = jnp.minimum(tiles, n_used - 1)
    last_tile_row = jnp.where(counts > 0, pend - TE, -1)
    rank = jnp.sum(onehot * (jnp.cumsum(onehot, axis=0) - 1), axis=1)
    pos = jnp.sum(onehot * pstart[None, :], axis=1) + rank
    experts = jnp.arange(N_EXPERTS, dtype=i32)
    nonempty = (counts > 0).astype(i32)
    seq_of_e = jnp.cumsum(nonempty) - 1
    n_seq = jnp.sum(nonempty)
    seq_e = jnp.sum(jnp.where((seq_of_e[None, :] == experts[:, None]) & (nonempty[None, :] > 0), experts[None, :], 0),
                    axis=1)
    tile_onehot = (tile_e[:, None] == experts[None, :]).astype(i32)
    tile_seq = jnp.sum(tile_onehot * seq_of_e[None, :], axis=1)
    tile_nt = jnp.maximum(jnp.sum(tile_onehot * (pcounts // TE)[None, :], axis=1), 1)
    as_i32 = lambda a: a.astype(i32)
    return dict(tile_e=as_i32(tile_e), tile_blk=as_i32(tile_blk), n_used=as_i32(n_used.reshape(1)),
                last_tile_row=as_i32(last_tile_row), pos=as_i32(pos), seq_e=as_i32(seq_e),
                n_seq=as_i32(n_seq.reshape(1)), tile_seq=as_i32(tile_seq), tile_nt=as_i32(tile_nt))


def _scatter_kernel(pos_ref, ltr_ref, hp_ref, xs_hbm, zero_scr, sem, zsem):
    i = pl.program_id(0)

    @pl.when(i == 0)
    def _():
        zero_scr[...] = jnp.zeros_like(zero_scr)

        def zero_copy(e):
            row0 = pl.multiple_of(ltr_ref[e], TE)
            return pltpu.make_async_copy(zero_scr, xs_hbm.at[pl.ds(row0, TE)], zsem)

        for e in range(N_EXPERTS):
            @pl.when(ltr_ref[e] >= 0)
            def _():
                zero_copy(e).start()
        for e in range(N_EXPERTS):
            @pl.when(ltr_ref[e] >= 0)
            def _():
                zero_copy(e).wait()

    base = i * (TT * TOP_K)

    def issue(r, carry):
        for k in range(TOP_K):
            p = pos_ref[base + TOP_K * r + k]
            pltpu.make_async_copy(hp_ref.at[pl.ds(r, 1)], xs_hbm.at[pl.ds(p, 1)], sem).start()
        return carry

    lax.fori_loop(0, TT, issue, 0)
    for k in range(TOP_K):
        pltpu.make_async_copy(hp_ref, xs_hbm.at[pl.ds(0, TT)], sem).wait()


def _dispatch(hp, pos, last_tile_row):
    grid_spec = pltpu.PrefetchScalarGridSpec(
        num_scalar_prefetch=2,
        grid=(N_TOK // TT,),
        in_specs=[pl.BlockSpec((TT, HALF_D), lambda i, pos, ltr: (i, 0))],
        out_specs=pl.BlockSpec(memory_space=pl.ANY),
        scratch_shapes=[pltpu.VMEM((TE, HALF_D), u32), pltpu.SemaphoreType.DMA(()), pltpu.SemaphoreType.DMA(())],
    )
    return pl.pallas_call(
        _scatter_kernel,
        grid_spec=grid_spec,
        out_shape=jax.ShapeDtypeStruct((R_MAX, HALF_D), u32),
        compiler_params=_params(1),
        name="dispatch",
    )(pos, last_tile_row, hp)


N_COL_HALVES = 2
RING_SLOTS = 4


def _grouped_kernel(tb_ref, nu_ref, seq_ref, nseq_ref, tseq_ref, tnt_ref, x_ref, w_hbm, o_ref, ring, wb, sem, nxt,
                    *, layer, k_dim, ck, part_cols, part_w, swiglu):
    del tb_ref
    j = pl.program_id(0)
    t = pl.program_id(1)
    nc = k_dim // ck
    nc_shift = nc.bit_length() - 1
    n_parts = len(part_cols)
    n_seq = nseq_ref[0]
    total = N_COL_HALVES * nc * n_seq

    def chunk_copy(g, p):
        blk = lax.shift_right_logical(g, nc_shift)
        c = g & (nc - 1)
        jj = (blk >= n_seq).astype(i32)
        e = seq_ref[blk - jj * n_seq]
        col = pl.multiple_of(part_cols[p] + jj * part_w, LANES)
        row = pl.multiple_of(c * ck, ck)
        slot = g & (RING_SLOTS - 1)
        return pltpu.make_async_copy(w_hbm.at[layer, e, pl.ds(row, ck), pl.ds(col, part_w)], ring.at[slot, p],
                                     sem.at[slot])

    def start_chunk(g):
        for p in range(n_parts):
            chunk_copy(g, p).start()

    def cast_chunk(g):
        for p in range(n_parts):
            chunk_copy(g, p).wait()
        buf = lax.shift_right_logical(g, nc_shift) & 1
        row = pl.multiple_of((g & (nc - 1)) * ck, ck)
        slot = g & (RING_SLOTS - 1)
        for p in range(n_parts):
            wb[buf, pl.ds(row, ck), p * part_w:(p + 1) * part_w] = ring[slot, p].astype(bf16)

        @pl.when(g + RING_SLOTS < total)
        def _():
            start_chunk(g + RING_SLOTS)

    def cast_next(n):
        def body(_, carry):
            cast_chunk(nxt[0])
            nxt[0] = nxt[0] + 1
            return carry

        lax.fori_loop(0, n, body, 0)

    @pl.when((j == 0) & (t == 0))
    def _():
        nxt[0] = 0
        for g in range(RING_SLOTS):
            start_chunk(g)
        cast_next(nc)

    @pl.when(t < nu_ref[0])
    def _():
        blk = j * n_seq + tseq_ref[t]
        w = wb.at[blk & 1]
        if swiglu:
            xw = x_ref[...]
            lo = lax.bitcast_convert_type(xw << 16, f32).astype(bf16)
            hi = lax.bitcast_convert_type(xw & jnp.uint32(0xFFFF0000), f32).astype(bf16)
            acc = (jnp.dot(lo, w[0:HALF_D, :], preferred_element_type=f32)
                   + jnp.dot(hi, w[HALF_D:D_MODEL, :], preferred_element_type=f32))
            o_ref[...] = (jax.nn.silu(acc[:, 0:part_w]) * acc[:, part_w:2 * part_w]).astype(o_ref.dtype)
        else:
            o_ref[...] = jnp.dot(x_ref[...], w[...], preferred_element_type=f32).astype(o_ref.dtype)
        nt = tnt_ref[t]
        share = lax.div(nc + nt - 1, nt)
        limit = jnp.minimum((blk + 2) * nc, total)
        cast_next(jnp.maximum(jnp.minimum(share, limit - nxt[0]), 0))


def _grouped_matmul(x, w, layer, meta, *, k_dim, ck, part_cols, part_w, swiglu, out_cols, out_dtype, name):
    n_parts = len(part_cols)
    grid_spec = pltpu.PrefetchScalarGridSpec(
        num_scalar_prefetch=6,
        grid=(N_COL_HALVES, N_ETILES),
        in_specs=[
            pl.BlockSpec((TE, x.shape[1]), lambda j, t, tb, *_: (tb[t], 0)),
            pl.BlockSpec(memory_space=pl.ANY),
        ],
        out_specs=pl.BlockSpec((TE, out_cols // N_COL_HALVES), lambda j, t, tb, *_: (tb[t], j)),
        scratch_shapes=[
            pltpu.VMEM((RING_SLOTS, n_parts, ck, part_w), f32),
            pltpu.VMEM((2, k_dim, n_parts * part_w), bf16),
            pltpu.SemaphoreType.DMA((RING_SLOTS,)),
            pltpu.SMEM((1,), i32),
        ],
    )
    return pl.pallas_call(
        functools.partial(_grouped_kernel, layer=layer, k_dim=k_dim, ck=ck, part_cols=part_cols, part_w=part_w,
                          swiglu=swiglu),
        grid_spec=grid_spec,
        out_shape=jax.ShapeDtypeStruct((R_MAX, out_cols), out_dtype),
        compiler_params=_params(2),
        name=name,
    )(meta["tile_blk"], meta["n_used"], meta["seq_e"], meta["n_seq"], meta["tile_seq"], meta["tile_nt"], x, w)


def _gate_up(xs, w_gate_up, layer, meta):
    half = D_EXPERT // N_COL_HALVES
    return _grouped_matmul(xs, w_gate_up, layer, meta, k_dim=D_MODEL, ck=512, part_cols=(0, D_EXPERT), part_w=half,
                           swiglu=True, out_cols=D_EXPERT, out_dtype=bf16, name="moe_gate_up")


def _down(h1, w_down, layer, meta):
    half = D_MODEL // N_COL_HALVES
    return _grouped_matmul(h1, w_down, layer, meta, k_dim=D_EXPERT, ck=256, part_cols=(0,), part_w=half,
                           swiglu=False, out_cols=D_MODEL, out_dtype=f32, name="moe_down")


def _combine_kernel(pos_ref, x_ref, gate_ref, r_ref, ys_hbm, o_ref, buf, sem, *, tok0, n_steps):
    i = pl.program_id(0)

    def issue(step, slot):
        base = (step * TC + tok0) * TOP_K

        def body(r, carry):
            for k in range(TOP_K):
                p = pos_ref[base + TOP_K * r + k]
                pltpu.make_async_copy(ys_hbm.at[pl.ds(p, 1)], buf.at[slot, k, pl.ds(r, 1)], sem.at[slot]).start()
            return carry

        lax.fori_loop(0, TC, body, 0)

    @pl.when(i == 0)
    def _():
        issue(0, 0)

    @pl.when(i + 1 < n_steps)
    def _():
        issue(i + 1, (i + 1) & 1)

    slot = i & 1
    for k in range(TOP_K):
        pltpu.make_async_copy(ys_hbm.at[pl.ds(0, TC)], buf.at[slot, k], sem.at[slot]).wait()
    r = r_ref[...]
    f = r[:, 2:3] * buf[slot, 0] + r[:, 3:4] * buf[slot, 1]
    o_ref[...] = x_ref[...] + gate_ref[...] * f


def _combine(x, mod, layer, route, ys, pos, *, tok0, n_tok):
    blk0 = tok0 // TC
    grid_spec = pltpu.PrefetchScalarGridSpec(
        num_scalar_prefetch=1,
        grid=(n_tok // TC,),
        in_specs=[
            pl.BlockSpec((TC, D_MODEL), lambda i, pos: (i + blk0, 0)),
            pl.BlockSpec((None, None, None, 1, D_MODEL), lambda i, pos: (layer, _mod_row(i + blk0, TC), 5, 0, 0)),
            pl.BlockSpec((TC, ROUTER_LANES), lambda i, pos: (i + blk0, 0)),
            pl.BlockSpec(memory_space=pl.ANY),
        ],
        out_specs=pl.BlockSpec((TC, D_MODEL), lambda i, pos: (i, 0)),
        scratch_shapes=[pltpu.VMEM((2, TOP_K, TC, D_MODEL), f32), pltpu.SemaphoreType.DMA((2,))],
    )
    return pl.pallas_call(
        functools.partial(_combine_kernel, tok0=tok0, n_steps=n_tok // TC),
        grid_spec=grid_spec,
        out_shape=jax.ShapeDtypeStruct((n_tok, D_MODEL), f32),
        compiler_params=_params(1),
        name="combine",
    )(pos, x, mod, route, ys)


def _hier_moe(x, mod, layer, gain, w_group, b_group, w_expert, b_expert, w_gate_up, w_down, *, split):
    hp, route = _router(x, mod, layer, gain, w_group, b_group, w_expert, b_expert)
    meta = _moe_layout(route)
    pos = meta["pos"]
    xs = _dispatch(hp, pos, meta["last_tile_row"])
    h1 = _gate_up(xs, w_gate_up, layer, meta)
    ys = _down(h1, w_down, layer, meta)
    if split:
        return (_combine(x, mod, layer, route, ys, pos, tok0=0, n_tok=N_CTX),
                _combine(x, mod, layer, route, ys, pos, tok0=N_CTX, n_tok=N_LAT))
    return _combine(x, mod, layer, route, ys, pos, tok0=0, n_tok=N_TOK)


def kernel(x_prompt, x_sample, cache_k, cache_v, c, c_ctx, norm_gain, w_mod, b_mod, attn_w_qkv, attn_q_gain, attn_k_gain, attn_w_o, cm_w_in, cm_v_gain, cm_w_s, cm_b_s, cm_w_out, moe_w_group, moe_b_group, moe_w_expert, moe_b_expert, moe_w_gate_up, moe_w_down):
    x_ctx = x_prompt.reshape(N_CTX, D_MODEL)
    x_lat = x_sample.reshape(N_LAT, D_MODEL)
    cond =jnp.concatenate([c_ctx[None, :], c, jnp.zeros((MOD_ROWS - 1 - DEC_BATCH, D_MODEL), f32)], axis=0)
    mod = _modulation(cond, w_mod, b_mod)

    w_qkv = attn_w_qkv[0].astype(bf16)
    q_gain = attn_q_gain[0][None, :]
    k_gain = attn_k_gain[0][None, :]
    gain0 = norm_gain[0, 0][None, :]
    qkv_ctx, kv_new = _qkv_proj(x_ctx, mod, 0, gain0, w_qkv, q_gain, k_gain, tok0=0, with_cache=True)
    (qkv_lat,) = _qkv_proj(x_lat, mod, 0, gain0, w_qkv, q_gain, k_gain, tok0=N_CTX, rope_tabs=_rope_tables())
    o_ctx = _attention(qkv_ctx, n_batch=BATCH, seq=SEQ, tq=SEQ)
    ck = cache_k[:, 0].reshape(DEC_BATCH, PAST_LEN, KV_DIM)
    cv = cache_v[:, 0].reshape(DEC_BATCH, PAST_LEN, KV_DIM)
    o_lat = _attention(qkv_lat, n_batch=DEC_BATCH, seq=DEC_SEQ, tq=TQ, cache=(ck, cv))
    w_o = attn_w_o[0].astype(bf16)
    x = _proj_res(o_ctx, w_o, x_ctx, mod, 0, tok0=0)
    x = _proj_res(o_lat, w_o, x_lat, mod, 0, tok0=N_CTX, out=x)
    x = _hier_moe(x, mod, 0, norm_gain[0, 1][None, :], moe_w_group[0], moe_b_group[0], moe_w_expert[0],
                  moe_b_expert[0], moe_w_gate_up, moe_w_down, split=False)

    uv, ssq = _chunk_in(x, mod, 1, norm_gain[1, 0][None, :], cm_w_in[0].astype(bf16))
    x = _chunk_out(uv, ssq, cm_v_gain[0][None, :], cm_w_s[0], cm_b_s[0][:, :, None], cm_w_out[0].astype(bf16),
                   x, mod, 1)
    y_ctx, y_lat = _hier_moe(x, mod, 1, norm_gain[1, 1][None, :], moe_w_group[1], moe_b_group[1], moe_w_expert[1],
                             moe_b_expert[1], moe_w_gate_up, moe_w_down, split=True)

    new_k = kv_new[:, :KV_DIM].reshape(BATCH, 1, SEQ, N_KV_HEADS, HEAD_DIM)
    new_v = kv_new[:, KV_DIM:].reshape(BATCH, 1, SEQ, N_KV_HEADS, HEAD_DIM)
    return (y_ctx.reshape(BATCH, SEQ, D_MODEL), y_lat.reshape(DEC_BATCH, DEC_SEQ, D_MODEL), new_k, new_v)
```

```python
import functools

import jax
import jax.numpy as jnp
from jax import lax
from jax.experimental import pallas as pl
from jax.experimental.pallas import tpu as pltpu

f32 = jnp.float32
bf16 = jnp.bfloat16
i32 = jnp.int32
u32 = jnp.uint32

D_MODEL = 4096
BATCH, SEQ = 16, 256
DEC_BATCH, DEC_SEQ = 4, 2048
PAST_LEN = 256
DEPTH = 2
GRID_W = 64
N_HEADS, N_KV_HEADS, HEAD_DIM = 32, 8, 128
GQA = N_HEADS // N_KV_HEADS
Q_DIM = N_HEADS * HEAD_DIM
KV_DIM = N_KV_HEADS * HEAD_DIM
QKV_DIM = Q_DIM + 2 * KV_DIM
AXIS_DIM = HEAD_DIM // 2
ROPE_THETA = 10000.0
CHUNK = 128
D_INNER = D_MODEL
N_SGU_GROUPS = 8
SGU_GROUP_DIM = D_INNER // N_SGU_GROUPS
N_EXPERT_GROUPS, EXPERTS_PER_GROUP = 4, 8
N_EXPERTS = N_EXPERT_GROUPS * EXPERTS_PER_GROUP
TOP_K = 2
D_EXPERT = D_MODEL // 4
N_MOD = 6
RMS_EPS = 1e-6

N_CTX = BATCH * SEQ
N_LAT = DEC_BATCH * DEC_SEQ
N_TOK = N_CTX + N_LAT
N_COPIES = N_TOK * TOP_K
MOD_ROWS = 8

LANES = 128
HALF_D = D_MODEL // 2

TM = 512
TM_PROJ = 1024
TN_QKV = 1024
TN_PROJ = 1024
TN_MOD = 512
TQ = 256
KEY_CHUNK = 128
TE = 256
N_ETILES = N_COPIES // TE + N_EXPERTS
R_MAX = N_ETILES * TE
TT = 512
TC = 256
ISSUE_UNROLL = 8
LOG2E = 1.4426950408889634
ROUTER_LANES = 128
EXPERT_LANE0 = 32

VMEM_LIMIT = 56 * 1024 * 1024


def _params(n_axes):
    return pltpu.CompilerParams(dimension_semantics=("arbitrary",) * n_axes, vmem_limit_bytes=VMEM_LIMIT)


def _mod_row(i, tm):
    start = i * tm
    return jnp.where(start < N_CTX, 0, 1 + (start - N_CTX) // DEC_SEQ)


def _mod_spec(layer, k, tm, tn=D_MODEL, col=None, row_fn=None):
    row_fn = row_fn or (lambda i: _mod_row(i, tm))
    if col is None:
        return pl.BlockSpec((None, None, None, 1, tn), lambda i, j: (layer, row_fn(i), k, 0, 0))
    return pl.BlockSpec((None, None, None, 1, tn), lambda i, j: (layer, row_fn(i), k, 0, j))


def _adaln(x, gain, shift, scale):
    ms = jnp.mean(x * x, axis=-1, keepdims=True)
    return x * lax.rsqrt(ms + RMS_EPS) * gain * (1.0 + scale) + shift


def _mod_kernel(c_ref, w_ref, b_ref, o_ref):
    s = jax.nn.silu(c_ref[...]).astype(bf16)
    o_ref[...] = jnp.dot(s, w_ref[...].astype(bf16), preferred_element_type=f32) + b_ref[...]


def _modulation(cond, w_mod, b_mod):
    n = N_MOD * D_MODEL
    out = pl.pallas_call(
        _mod_kernel,
        grid=(DEPTH, n // TN_MOD),
        in_specs=[
            pl.BlockSpec((MOD_ROWS, D_MODEL), lambda l, j: (0, 0)),
            pl.BlockSpec((None, D_MODEL, TN_MOD), lambda l, j: (l, 0, j)),
            pl.BlockSpec((None, 1, TN_MOD), lambda l, j: (l, 0, j)),
        ],
        out_specs=pl.BlockSpec((None, MOD_ROWS, TN_MOD), lambda l, j: (l, 0, j)),
        out_shape=jax.ShapeDtypeStruct((DEPTH, MOD_ROWS, n), f32),
        compiler_params=_params(2),
        name="modulation",
    )(cond, w_mod, b_mod.reshape(DEPTH, 1, n))
    return out.reshape(DEPTH, MOD_ROWS, N_MOD, 1, D_MODEL)


def _qkv_kernel(*refs, rope, with_cache):
    x_ref, g_ref, sh_ref, sc_ref, w_ref, hg_ref = refs[:6]
    pos = 6
    if rope:
        cos_ref, sa_ref, sb_ref = refs[pos:pos + 3]
        pos += 3
    o_ref = refs[pos]
    pos += 1
    if with_cache:
        kv_ref = refs[pos]
        pos += 1
    h_scr = refs[pos]

    j = pl.program_id(1)
    nq = Q_DIM // TN_QKV
    nk = KV_DIM // TN_QKV

    @pl.when(j == 0)
    def _():
        h_scr[...] = _adaln(x_ref[...], g_ref[...], sh_ref[...], sc_ref[...]).astype(bf16)

    acc = jnp.dot(h_scr[...], w_ref[...], preferred_element_type=f32)

    is_v = j >= nq + nk
    gain = hg_ref[...]
    for hh in range(TN_QKV // HEAD_DIM):
        cols = slice(hh * HEAD_DIM, (hh + 1) * HEAD_DIM)
        a = acc[:, cols]
        y = a * lax.rsqrt(jnp.mean(a * a, axis=-1, keepdims=True) + RMS_EPS) * gain
        if rope:
            y = (y * cos_ref[...] + pltpu.roll(y, HEAD_DIM - AXIS_DIM // 2, 1) * sa_ref[...]
                 + pltpu.roll(y, AXIS_DIM // 2, 1) * sb_ref[...])
        y = jnp.where(is_v, a, y)
        o_ref[:, cols] = y.astype(bf16)
        if with_cache:
            kv_ref[:, cols] = y


def _qkv_proj(x, mod, layer, gain, w_qkv, q_gain, k_gain, *, tok0, rope_tabs=None, with_cache=False):
    n_tok = x.shape[0]
    blk0 = tok0 // TM
    nj = QKV_DIM // TN_QKV
    nq = Q_DIM // TN_QKV
    nk = KV_DIM // TN_QKV
    head_gain = jnp.stack([q_gain * (HEAD_DIM ** -0.5 * LOG2E)] * nq + [k_gain] * nk
                          + [jnp.ones_like(k_gain)] * (nj - nq - nk))
    row_fn = lambda i: _mod_row(i + blk0, TM)
    in_specs = [
        pl.BlockSpec((TM, D_MODEL), lambda i, j: (i, 0)),
        pl.BlockSpec((1, D_MODEL), lambda i, j: (0, 0)),
        _mod_spec(layer, 0, TM, row_fn=row_fn),
        _mod_spec(layer, 1, TM, row_fn=row_fn),
        pl.BlockSpec((D_MODEL, TN_QKV), lambda i, j: (0, j)),
        pl.BlockSpec((None, 1, HEAD_DIM), lambda i, j: (j, 0, 0)),
    ]
    args = [x, gain, mod, mod, w_qkv, head_gain]
    if rope_tabs is not None:
        nb = DEC_SEQ // TM
        in_specs += [pl.BlockSpec((TM, HEAD_DIM), lambda i, j: (i % nb, 0))] * 3
        args += list(rope_tabs)
    out_specs = [pl.BlockSpec((TM, TN_QKV), lambda i, j: (i, j))]
    out_shape = [jax.ShapeDtypeStruct((n_tok, QKV_DIM), bf16)]
    if with_cache:
        out_specs.append(pl.BlockSpec((TM, TN_QKV), lambda i, j: (i, jnp.maximum(j - nq, 0))))
        out_shape.append(jax.ShapeDtypeStruct((n_tok, 2 * KV_DIM), f32))
    return pl.pallas_call(
        functools.partial(_qkv_kernel, rope=rope_tabs is not None, with_cache=with_cache),
        grid=(n_tok // TM, nj),
        in_specs=in_specs,
        out_specs=out_specs,
        out_shape=out_shape,
        scratch_shapes=[pltpu.VMEM((TM, D_MODEL), bf16)],
        compiler_params=_params(2),
        name="qkv_rope" if rope_tabs is not None else "qkv_ctx",
    )(*args)


def _rope_tables():
    t = jnp.arange(DEC_SEQ)
    row = (t // GRID_W).astype(f32)
    col = (t % GRID_W).astype(f32)
    inv_freq = ROPE_THETA ** (-jnp.arange(0, AXIS_DIM, 2, dtype=f32) / AXIS_DIM)
    lane = jnp.arange(HEAD_DIM)
    freq = inv_freq[lane % (AXIS_DIM // 2)]
    pos = jnp.where(lane[None, :] < AXIS_DIM, row[:, None], col[:, None])
    ang = pos * freq[None, :]
    first = (lane % AXIS_DIM) < (AXIS_DIM // 2)
    sin = jnp.sin(ang)
    return jnp.cos(ang), jnp.where(first[None, :], -sin, 0.0), jnp.where(first[None, :], 0.0, sin)


def _attn_kernel(*refs, with_cache):
    q_ref, k_ref, v_ref = refs[:3]
    if with_cache:
        ck_ref, cv_ref, o_ref, k_all, vt_all = refs[3:8]
    else:
        o_ref, k_all, vt_all = refs[3:6]
    n_keys = k_all.shape[0]
    kc = KEY_CHUNK

    @pl.when(pl.program_id(2) == 0)
    def _():
        off = 0
        if with_cache:
            k_all[0:PAST_LEN, :] = ck_ref[...].astype(bf16)
            for c in range(PAST_LEN // kc):
                vt_all[:, c * kc:(c + 1) * kc] = cv_ref[c * kc:(c + 1) * kc, :].T.astype(bf16)
            off = PAST_LEN
        k_all[off:n_keys, :] = k_ref[...]
        for c in range((n_keys - off) // kc):
            vt_all[:, off + c * kc:off + (c + 1) * kc] = v_ref[c * kc:(c + 1) * kc, :].astype(f32).T.astype(bf16)

    nt_dims = (((1,), (1,)), ((), ()))
    for g in range(GQA):
        qg = q_ref[:, g * HEAD_DIM:(g + 1) * HEAD_DIM]
        m = l = acc = None
        for c in range(n_keys // kc):
            st = lax.dot_general(k_all[c * kc:(c + 1) * kc, :], qg, nt_dims, preferred_element_type=f32)
            ms = jnp.max(st, axis=0, keepdims=True)
            vt = vt_all[:, c * kc:(c + 1) * kc]
            if m is None:
                m = ms
                p = jnp.exp2(st - m)
                l = jnp.sum(p, axis=0, keepdims=True)
                acc = jnp.dot(vt, p.astype(bf16), preferred_element_type=f32)
            else:
                m_new = jnp.maximum(m, ms)
                alpha = jnp.exp2(m - m_new)
                p = jnp.exp2(st - m_new)
                l = alpha * l + jnp.sum(p, axis=0, keepdims=True)
                acc = alpha * acc + jnp.dot(vt, p.astype(bf16), preferred_element_type=f32)
                m = m_new
        o_ref[:, g * HEAD_DIM:(g + 1) * HEAD_DIM] = (acc / l).T.astype(bf16)


def _attention(qkv, *, n_batch, seq, tq, cache=None):
    nqb = seq // tq
    n_keys = seq + (PAST_LEN if cache is not None else 0)
    kcol = Q_DIM // HEAD_DIM
    vcol = (Q_DIM + KV_DIM) // HEAD_DIM
    in_specs = [
        pl.BlockSpec((tq, GQA * HEAD_DIM), lambda b, h, i: (b * nqb + i, h)),
        pl.BlockSpec((seq, HEAD_DIM), lambda b, h, i: (b, kcol + h)),
        pl.BlockSpec((seq, HEAD_DIM), lambda b, h, i: (b, vcol + h)),
    ]
    args = [qkv, qkv, qkv]
    if cache is not None:
        in_specs += [pl.BlockSpec((None, PAST_LEN, HEAD_DIM), lambda b, h, i: (b, 0, h))] * 2
        args += list(cache)
    return pl.pallas_call(
        functools.partial(_attn_kernel, with_cache=cache is not None),
        grid=(n_batch, N_KV_HEADS, nqb),
        in_specs=in_specs,
        out_specs=pl.BlockSpec((tq, GQA * HEAD_DIM), lambda b, h, i: (b * nqb + i, h)),
        out_shape=jax.ShapeDtypeStruct((n_batch * seq, Q_DIM), bf16),
        scratch_shapes=[pltpu.VMEM((n_keys, HEAD_DIM), bf16), pltpu.VMEM((HEAD_DIM, n_keys), bf16)],
        compiler_params=_params(3),
        name="attn_lat" if cache is not None else "attn_ctx",
    )(*args)


def _proj_res_kernel(a_ref, w_ref, x_ref, gate_ref, *rest):
    o_ref = rest[-1]
    o_ref[...] = x_ref[...] + gate_ref[...] * jnp.dot(a_ref[...], w_ref[...], preferred_element_type=f32)


def _proj_res(a, w, x, mod, layer, *, tok0, out=None):
    n_tok = a.shape[0]
    tm = TM_PROJ
    blk0 = tok0 // tm
    row_fn = lambda i: _mod_row(i + blk0, tm)
    in_specs = [
        pl.BlockSpec((tm, a.shape[1]), lambda i, j: (i, 0)),
        pl.BlockSpec((a.shape[1], TN_PROJ), lambda i, j: (0, j)),
        pl.BlockSpec((tm, TN_PROJ), lambda i, j: (i, j)),
        _mod_spec(layer, 2, tm, tn=TN_PROJ, col=True, row_fn=row_fn),
    ]
    args = [a, w, x, mod]
    aliases = {}
    if out is not None:
        in_specs.append(pl.BlockSpec(memory_space=pl.ANY))
        args.append(out)
        aliases = {len(args) - 1: 0}
    return pl.pallas_call(
        _proj_res_kernel,
        grid=(n_tok // tm, D_MODEL // TN_PROJ),
        in_specs=in_specs,
        out_specs=pl.BlockSpec((tm, TN_PROJ), lambda i, j: (i + blk0, j)),
        out_shape=jax.ShapeDtypeStruct((N_TOK, D_MODEL), f32),
        input_output_aliases=aliases,
        compiler_params=_params(2),
        name="proj_res",
    )(*args)


def _win_kernel(h_ref, w_ref, o_ref, ssq_ref, *, tn):
    j = pl.program_id(1)
    nu = D_INNER // tn

    @pl.when(j == 0)
    def _():
        ssq_ref[...] = jnp.zeros_like(ssq_ref)

    y = jax.nn.gelu(jnp.dot(h_ref[...], w_ref[...], preferred_element_type=f32))
    o_ref[...] = y.astype(bf16)

    y2 = y * y
    part = y2[:, 0:LANES]
    for c in range(1, tn // LANES):
        part = part + y2[:, c * LANES:(c + 1) * LANES]
    ssq_ref[...] += jnp.where(j >= nu, part, 0.0)


def _chunk_in(h, w_in):
    tn = TN_QKV
    tm = TM_PROJ
    return pl.pallas_call(
        functools.partial(_win_kernel, tn=tn),
        grid=(N_TOK // tm, 2 * D_INNER // tn),
        in_specs=[
            pl.BlockSpec((tm, D_MODEL), lambda i, j: (i, 0)),
            pl.BlockSpec((D_MODEL, tn), lambda i, j: (0, j)),
        ],
        out_specs=[
            pl.BlockSpec((tm, tn), lambda i, j: (i, j)),
            pl.BlockSpec((tm, LANES), lambda i, j: (i, 0)),
        ],
        out_shape=[jax.ShapeDtypeStruct((N_TOK, 2 * D_INNER), bf16), jax.ShapeDtypeStruct((N_TOK, LANES), f32)],
        compiler_params=_params(2),
        name="chunk_in",
    )(h, w_in)


def _sgu_out_kernel(u_ref, v_ref, ssq_ref, vg_ref, ws_ref, bs_ref, w_ref, x_ref, gate_ref, o_ref, a_scr):
    j = pl.program_id(1)

    @pl.when(j == 0)
    def _():
        rstd = lax.rsqrt(jnp.sum(ssq_ref[...], axis=-1, keepdims=True) * (1.0 / D_INNER) + RMS_EPS)
        for c in range(TM // CHUNK):
            rows = slice(c * CHUNK, (c + 1) * CHUNK)
            rs = rstd[rows]
            for g in range(N_SGU_GROUPS):
                cols = slice(g * SGU_GROUP_DIM, (g + 1) * SGU_GROUP_DIM)
                vn = (v_ref[rows, cols].astype(f32) * rs * vg_ref[:, cols]).astype(bf16)
                vm = jnp.dot(ws_ref[g].astype(bf16), vn, preferred_element_type=f32) + bs_ref[g]
                a_scr[rows, cols] = (u_ref[rows, cols].astype(f32) * vm).astype(bf16)

    o_ref[...] = x_ref[...] + gate_ref[...] * jnp.dot(a_scr[...], w_ref[...], preferred_element_type=f32)


def _chunk_out(uv, ssq, v_gain, w_s, b_s, w_out, x, mod, layer):
    return pl.pallas_call(
        _sgu_out_kernel,
        grid=(N_TOK // TM, D_MODEL // TN_PROJ),
        in_specs=[
            pl.BlockSpec((TM, D_INNER), lambda i, j: (i, 0)),
            pl.BlockSpec((TM, D_INNER), lambda i, j: (i, 1)),
            pl.BlockSpec((TM, LANES), lambda i, j: (i, 0)),
            pl.BlockSpec((1, D_INNER), lambda i, j: (0, 0)),
            pl.BlockSpec((N_SGU_GROUPS, CHUNK, CHUNK), lambda i, j: (0, 0, 0)),
            pl.BlockSpec((N_SGU_GROUPS, CHUNK, 1), lambda i, j: (0, 0, 0)),
            pl.BlockSpec((D_INNER, TN_PROJ), lambda i, j: (0, j)),
            pl.BlockSpec((TM, TN_PROJ), lambda i, j: (i, j)),
            _mod_spec(layer, 2, TM, tn=TN_PROJ, col=True),
        ],
        out_specs=pl.BlockSpec((TM, TN_PROJ), lambda i, j: (i, j)),
        out_shape=jax.ShapeDtypeStruct((N_TOK, D_MODEL), f32),
        scratch_shapes=[pltpu.VMEM((TM, D_INNER), bf16)],
        compiler_params=_params(2),
        name="sgu_out",
    )(uv, uv, ssq, v_gain, w_s, b_s, w_out, x, mod)


def _router_kernel(x_ref, g_ref, sh_ref, sc_ref, whi_ref, wlo_ref, b_ref, hp_ref, r_ref):
    h = _adaln(x_ref[...], g_ref[...], sh_ref[...], sc_ref[...])
    hb = h.astype(bf16)
    hf = hb.astype(f32)
    lo = (h - hf).astype(bf16)
    logits = (jnp.dot(hb, whi_ref[...], preferred_element_type=f32)
              + (jnp.dot(lo, whi_ref[...], preferred_element_type=f32)
                 + jnp.dot(hb, wlo_ref[...], preferred_element_type=f32))) + b_ref[...]

    bits = lax.bitcast_convert_type(hf, u32)
    hp_ref[...] = (bits[:, :HALF_D] >> 16) | bits[:, HALF_D:]

    lane = lax.broadcasted_iota(i32, logits.shape, 1).astype(f32)
    neg = -jnp.inf
    big = float(ROUTER_LANES)
    lg = jnp.where(lane < N_EXPERT_GROUPS, logits, neg)
    mg = jnp.max(lg, axis=-1, keepdims=True)
    pg_top = 1.0 / jnp.sum(jnp.exp(lg - mg), axis=-1, keepdims=True)
    grp = jnp.min(jnp.where(lg == mg, lane, big), axis=-1, keepdims=True)
    lane0 = EXPERT_LANE0 + EXPERTS_PER_GROUP * grp
    le = jnp.where((lane >= lane0) & (lane < lane0 + EXPERTS_PER_GROUP), logits, neg)
    m1 = jnp.max(le, axis=-1, keepdims=True)
    i1 = jnp.min(jnp.where(le == m1, lane, big), axis=-1, keepdims=True)
    le2 = jnp.where(lane == i1, neg, le)
    m2 = jnp.max(le2, axis=-1, keepdims=True)
    i2 = jnp.min(jnp.where(le2 == m2, lane, big), axis=-1, keepdims=True)
    e2 = jnp.exp(m2 - m1)
    p1 = 1.0 / (1.0 + e2)
    p2 = e2 * p1
    r_ref[...] = jnp.where(lane == 0, i1 - EXPERT_LANE0,
                           jnp.where(lane == 1, i2 - EXPERT_LANE0,
                                     jnp.where(lane == 2, pg_top * p1,
                                               jnp.where(lane == 3, pg_top * p2, 0.0))))


def _router(x, mod, layer, gain, w_group, b_group, w_expert, b_expert):
    tm = 256
    pad0 = jnp.zeros((D_MODEL, EXPERT_LANE0 - N_EXPERT_GROUPS), f32)
    pad1 = jnp.zeros((D_MODEL, ROUTER_LANES - EXPERT_LANE0 - N_EXPERTS), f32)
    w_r = jnp.concatenate([w_group, pad0, w_expert, pad1], axis=1)
    w_hi = w_r.astype(bf16)
    w_lo = (w_r - w_hi.astype(f32)).astype(bf16)
    b_r = jnp.concatenate([b_group, pad0[0], b_expert, pad1[0]])[None, :]
    return pl.pallas_call(
        _router_kernel,
        grid=(N_TOK // tm, 1),
        in_specs=[
            pl.BlockSpec((tm, D_MODEL), lambda i, j: (i, 0)),
            pl.BlockSpec((1, D_MODEL), lambda i, j: (0, 0)),
            _mod_spec(layer, 3, tm),
            _mod_spec(layer, 4, tm),
            pl.BlockSpec((D_MODEL, ROUTER_LANES), lambda i, j: (0, 0)),
            pl.BlockSpec((D_MODEL, ROUTER_LANES), lambda i, j: (0, 0)),
            pl.BlockSpec((1, ROUTER_LANES), lambda i, j: (0, 0)),
        ],
        out_specs=[
            pl.BlockSpec((tm, HALF_D), lambda i, j: (i, 0)),
            pl.BlockSpec((tm, ROUTER_LANES), lambda i, j: (i, 0)),
        ],
        out_shape=[jax.ShapeDtypeStruct((N_TOK, HALF_D), u32), jax.ShapeDtypeStruct((N_TOK, ROUTER_LANES), f32)],
        compiler_params=_params(2),
        name="router",
    )(x, gain, mod, mod, w_hi, w_lo, b_r)


def _moe_layout(route):
    flat_e = route[:, 0:TOP_K].astype(i32).reshape(-1)
    onehot = (flat_e[:, None] == jnp.arange(N_EXPERTS, dtype=i32)[None, :]).astype(i32)
    counts = jnp.sum(onehot, axis=0)
    pcounts = ((counts + TE - 1) // TE) * TE
    pend = jnp.cumsum(pcounts)
    pstart = pend - pcounts
    n_used = pend[-1] // TE
    tiles = jnp.arange(N_ETILES, dtype=i32)
    tile_e = jnp.sum((tiles[:, None] * TE >= pend[None, :]).astype(i32), axis=1)
    last_e = jnp.max(jnp.where(tiles < n_used, tile_e, 0))
    tile_e = jnp.minimum(tile_e, last_e)
    tile_blk = jnp.minimum(tiles, n_used - 1)
    last_tile_row = jnp.where(counts > 0, pend - TE, -1)
    rank = jnp.sum(onehot * (jnp.cumsum(onehot, axis=0) - 1), axis=1)
    pos = jnp.sum(onehot * pstart[None, :], axis=1) + rank
    experts = jnp.arange(N_EXPERTS, dtype=i32)
    nonempty = (counts > 0).astype(i32)
    seq_of_e = jnp.cumsum(nonempty) - 1
    n_seq = jnp.sum(nonempty)
    seq_e = jnp.sum(jnp.where((seq_of_e[None, :] == experts[:, None]) & (nonempty[None, :] > 0), experts[None, :], 0),
                    axis=1)
    tile_onehot = (tile_e[:, None] == experts[None, :]).astype(i32)
    tile_seq = jnp.sum(tile_onehot * seq_of_e[None, :], axis=1)
    tile_nt = jnp.maximum(jnp.sum(tile_onehot * (pcounts // TE)[None, :], axis=1), 1)
    as_i32 = lambda a: a.astype(i32)
    return dict(tile_e=as_i32(tile_e), tile_blk=as_i32(tile_blk), n_used=as_i32(n_used.reshape(1)),
                last_tile_row=as_i32(last_tile_row), pos=as_i32(pos), seq_e=as_i32(seq_e),
                n_seq=as_i32(n_seq.reshape(1)), tile_seq=as_i32(tile_seq), tile_nt=as_i32(tile_nt))


def _scatter_kernel(pos_ref, ltr_ref, hp_ref, xs_hbm, zero_scr, sem, zsem):
    i = pl.program_id(0)

    @pl.when(i == 0)
    def _():
        zero_scr[...] = jnp.zeros_like(zero_scr)

        def zero_copy(e):
            row0 = pl.multiple_of(ltr_ref[e], TE)
            return pltpu.make_async_copy(zero_scr, xs_hbm.at[pl.ds(row0, TE)], zsem)

        for e in range(N_EXPERTS):
            @pl.when(ltr_ref[e] >= 0)
            def _():
                zero_copy(e).start()
        for e in range(N_EXPERTS):
            @pl.when(ltr_ref[e] >= 0)
            def _():
                zero_copy(e).wait()

    base = i * (TT * TOP_K)

    def issue(r, carry):
        for k in range(TOP_K):
            p = pos_ref[base + TOP_K * r + k]
            pltpu.make_async_copy(hp_ref.at[pl.ds(r, 1)], xs_hbm.at[pl.ds(p, 1)], sem).start()
        return carry

    lax.fori_loop(0, TT, issue, 0, unroll=ISSUE_UNROLL)
    for k in range(TOP_K):
        pltpu.make_async_copy(hp_ref, xs_hbm.at[pl.ds(0, TT)], sem).wait()


def _dispatch(hp, pos, last_tile_row):
    grid_spec = pltpu.PrefetchScalarGridSpec(
        num_scalar_prefetch=2,
        grid=(N_TOK // TT,),
        in_specs=[pl.BlockSpec((TT, HALF_D), lambda i, pos, ltr: (i, 0))],
        out_specs=pl.BlockSpec(memory_space=pl.ANY),
        scratch_shapes=[pltpu.VMEM((TE, HALF_D), u32), pltpu.SemaphoreType.DMA(()), pltpu.SemaphoreType.DMA(())],
    )
    return pl.pallas_call(
        _scatter_kernel,
        grid_spec=grid_spec,
        out_shape=jax.ShapeDtypeStruct((R_MAX, HALF_D), u32),
        compiler_params=_params(1),
        name="dispatch",
    )(pos, last_tile_row, hp)


RING_SLOTS = 4
CHUNK_ROWS = 128


def _grouped_kernel(tb_ref, nu_ref, seq_ref, nseq_ref, tseq_ref, tnt_ref, x_ref, w_hbm, o_ref, ring, wb, sem, nxt,
                    *, layer, k_dim, swiglu):
    del tb_ref
    t = pl.program_id(0)
    nc = k_dim // CHUNK_ROWS
    nc_shift = nc.bit_length() - 1
    total = nc * nseq_ref[0]

    def chunk_copy(g):
        e = seq_ref[lax.shift_right_logical(g, nc_shift)]
        row = pl.multiple_of((g & (nc - 1)) * CHUNK_ROWS, CHUNK_ROWS)
        slot = g & (RING_SLOTS - 1)
        return pltpu.make_async_copy(w_hbm.at[layer, e, pl.ds(row, CHUNK_ROWS)], ring.at[slot], sem.at[slot])

    def cast_chunk(g):
        chunk_copy(g).wait()
        buf = lax.shift_right_logical(g, nc_shift) & 1
        row = pl.multiple_of((g & (nc - 1)) * CHUNK_ROWS, CHUNK_ROWS)
        wb[buf, pl.ds(row, CHUNK_ROWS), :] = ring[g & (RING_SLOTS - 1)].astype(bf16)

        @pl.when(g + RING_SLOTS < total)
        def _():
            chunk_copy(g + RING_SLOTS).start()

    def cast_next(n):
        def body(_, carry):
            cast_chunk(nxt[0])
            nxt[0] = nxt[0] + 1
            return carry

        lax.fori_loop(0, n, body, 0)

    @pl.when(t == 0)
    def _():
        nxt[0] = 0
        for g in range(RING_SLOTS):
            chunk_copy(g).start()
        cast_next(nc)

    @pl.when(t < nu_ref[0])
    def _():
        s = tseq_ref[t]
        w = wb.at[s & 1]
        if swiglu:
            xw = x_ref[...]
            lo = lax.bitcast_convert_type(xw << 16, f32).astype(bf16)
            hi = lax.bitcast_convert_type(xw & jnp.uint32(0xFFFF0000), f32).astype(bf16)
            acc = (jnp.dot(lo, w[0:HALF_D, :], preferred_element_type=f32)
                   + jnp.dot(hi, w[HALF_D:D_MODEL, :], preferred_element_type=f32))
            o_ref[...] = (jax.nn.silu(acc[:, 0:D_EXPERT]) * acc[:, D_EXPERT:2 * D_EXPERT]).astype(o_ref.dtype)
        else:
            o_ref[...] = jnp.dot(x_ref[...], w[...], preferred_element_type=f32).astype(o_ref.dtype)
        nt = tnt_ref[t]
        share = lax.div(nc + nt - 1, nt)
        limit = jnp.minimum((s + 2) * nc, total)
        cast_next(jnp.maximum(jnp.minimum(share, limit - nxt[0]), 0))


def _grouped_matmul(x, w, layer, meta, *, swiglu, out_cols, out_dtype, name):
    k_dim, n_dim = w.shape[2], w.shape[3]
    grid_spec = pltpu.PrefetchScalarGridSpec(
        num_scalar_prefetch=6,
        grid=(N_ETILES,),
        in_specs=[
            pl.BlockSpec((TE, x.shape[1]), lambda t, tb, *_: (tb[t], 0)),
            pl.BlockSpec(memory_space=pl.ANY),
        ],
        out_specs=pl.BlockSpec((TE, out_cols), lambda t, tb, *_: (tb[t], 0)),
        scratch_shapes=[
            pltpu.VMEM((RING_SLOTS, CHUNK_ROWS, n_dim), f32),
            pltpu.VMEM((2, k_dim, n_dim), bf16),
            pltpu.SemaphoreType.DMA((RING_SLOTS,)),
            pltpu.SMEM((1,), i32),
        ],
    )
    return pl.pallas_call(
        functools.partial(_grouped_kernel, layer=layer, k_dim=k_dim, swiglu=swiglu),
        grid_spec=grid_spec,
        out_shape=jax.ShapeDtypeStruct((R_MAX, out_cols), out_dtype),
        compiler_params=_params(1),
        name=name,
    )(meta["tile_blk"], meta["n_used"], meta["seq_e"], meta["n_seq"], meta["tile_seq"], meta["tile_nt"], x, w)


def _gate_up(xs, w_gate_up, layer, meta):
    return _grouped_matmul(xs, w_gate_up, layer, meta, swiglu=True, out_cols=D_EXPERT, out_dtype=bf16,
                           name="moe_gate_up")


def _down(h1, w_down, layer, meta):
    return _grouped_matmul(h1, w_down, layer, meta, swiglu=False, out_cols=D_MODEL, out_dtype=f32, name="moe_down")


def _combine_kernel(pos_ref, x_ref, gate_ref, r_ref, ys_hbm, *rest, tok0, n_steps, next_norm):
    if next_norm:
        ng_ref, nsh_ref, nsc_ref, o_ref, h_ref, buf, sem = rest
    else:
        o_ref, buf, sem = rest
    i = pl.program_id(0)

    def issue(step, slot):
        base = (step * TC + tok0) * TOP_K

        def body(r, carry):
            for k in range(TOP_K):
                p = pos_ref[base + TOP_K * r + k]
                pltpu.make_async_copy(ys_hbm.at[pl.ds(p, 1)], buf.at[slot, k, pl.ds(r, 1)], sem.at[slot]).start()
            return carry

        lax.fori_loop(0, TC, body, 0, unroll=ISSUE_UNROLL)

    @pl.when(i == 0)
    def _():
        issue(0, 0)

    @pl.when(i + 1 < n_steps)
    def _():
        issue(i + 1, (i + 1) & 1)

    slot = i & 1
    for k in range(TOP_K):
        pltpu.make_async_copy(ys_hbm.at[pl.ds(0, TC)], buf.at[slot, k], sem.at[slot]).wait()
    r = r_ref[...]
    f = r[:, 2:3] * buf[slot, 0] + r[:, 3:4] * buf[slot, 1]
    y = x_ref[...] + gate_ref[...] * f
    o_ref[...] = y
    if next_norm:
        h_ref[...] = _adaln(y, ng_ref[...], nsh_ref[...], nsc_ref[...]).astype(bf16)


def _combine(x, mod, layer, route, ys, pos, *, tok0, n_tok, next_gain=None):
    blk0 = tok0 // TC
    mod_spec = lambda lyr, k: pl.BlockSpec((None, None, None, 1, D_MODEL),
                                           lambda i, pos: (lyr, _mod_row(i + blk0, TC), k, 0, 0))
    in_specs = [
        pl.BlockSpec((TC, D_MODEL), lambda i, pos: (i + blk0, 0)),
        mod_spec(layer, 5),
        pl.BlockSpec((TC, ROUTER_LANES), lambda i, pos: (i + blk0, 0)),
        pl.BlockSpec(memory_space=pl.ANY),
    ]
    args = [pos, x, mod, route, ys]
    out_specs = [pl.BlockSpec((TC, D_MODEL), lambda i, pos: (i, 0))]
    out_shape = [jax.ShapeDtypeStruct((n_tok, D_MODEL), f32)]
    if next_gain is not None:
        in_specs += [pl.BlockSpec((1, D_MODEL), lambda i, pos: (0, 0)), mod_spec(layer + 1, 0), mod_spec(layer + 1, 1)]
        args += [next_gain, mod, mod]
        out_specs.append(pl.BlockSpec((TC, D_MODEL), lambda i, pos: (i, 0)))
        out_shape.append(jax.ShapeDtypeStruct((n_tok, D_MODEL), bf16))
    grid_spec = pltpu.PrefetchScalarGridSpec(
        num_scalar_prefetch=1,
        grid=(n_tok // TC,),
        in_specs=in_specs,
        out_specs=out_specs,
        scratch_shapes=[pltpu.VMEM((2, TOP_K, TC, D_MODEL), f32), pltpu.SemaphoreType.DMA((2,))],
    )
    out = pl.pallas_call(
        functools.partial(_combine_kernel, tok0=tok0, n_steps=n_tok // TC, next_norm=next_gain is not None),
        grid_spec=grid_spec,
        out_shape=out_shape,
        compiler_params=_params(1),
        name="combine",
    )(*args)
    return out if next_gain is not None else out[0]


def _hier_moe(x, mod, layer, gain, w_group, b_group, w_expert, b_expert, w_gate_up, w_down, *, next_gain=None):
    hp, route = _router(x, mod, layer, gain, w_group, b_group, w_expert, b_expert)
    meta = _moe_layout(route)
    pos = meta["pos"]
    xs = _dispatch(hp, pos, meta["last_tile_row"])
    h1 = _gate_up(xs, w_gate_up, layer, meta)
    ys = _down(h1, w_down, layer, meta)
    if next_gain is None:
        return (_combine(x, mod, layer, route, ys, pos, tok0=0, n_tok=N_CTX),
                _combine(x, mod, layer, route, ys, pos, tok0=N_CTX, n_tok=N_LAT))
    return _combine(x, mod, layer, route, ys, pos, tok0=0, n_tok=N_TOK, next_gain=next_gain)


def kernel(x_prompt, x_sample, cache_k, cache_v, c, c_ctx, norm_gain, w_mod, b_mod, attn_w_qkv, attn_q_gain, attn_k_gain, attn_w_o, cm_w_in, cm_v_gain, cm_w_s, cm_b_s, cm_w_out, moe_w_group, moe_b_group, moe_w_expert, moe_b_expert, moe_w_gate_up, moe_w_down):
    x_ctx = x_prompt.reshape(N_CTX, D_MODEL)
    x_lat = x_sample.reshape(N_LAT, D_MODEL)
    cond =jnp.concatenate([c_ctx[None, :], c, jnp.zeros((MOD_ROWS - 1 - DEC_BATCH, D_MODEL), f32)], axis=0)
    mod = _modulation(cond, w_mod, b_mod)

    w_qkv = attn_w_qkv[0].astype(bf16)
    q_gain = attn_q_gain[0][None, :]
    k_gain = attn_k_gain[0][None, :]
    gain0 = norm_gain[0, 0][None, :]
    qkv_ctx, kv_new = _qkv_proj(x_ctx, mod, 0, gain0, w_qkv, q_gain, k_gain, tok0=0, with_cache=True)
    (qkv_lat,) = _qkv_proj(x_lat, mod, 0, gain0, w_qkv, q_gain, k_gain, tok0=N_CTX, rope_tabs=_rope_tables())
    o_ctx = _attention(qkv_ctx, n_batch=BATCH, seq=SEQ, tq=SEQ)
    ck = cache_k[:, 0].reshape(DEC_BATCH, PAST_LEN, KV_DIM)
    cv = cache_v[:, 0].reshape(DEC_BATCH, PAST_LEN, KV_DIM)
    o_lat = _attention(qkv_lat, n_batch=DEC_BATCH, seq=DEC_SEQ, tq=TQ, cache=(ck, cv))
    w_o = attn_w_o[0].astype(bf16)
    x = _proj_res(o_ctx, w_o, x_ctx, mod, 0, tok0=0)
    x = _proj_res(o_lat, w_o, x_lat, mod, 0, tok0=N_CTX, out=x)
    x, h = _hier_moe(x, mod, 0, norm_gain[0, 1][None, :], moe_w_group[0], moe_b_group[0], moe_w_expert[0],
                     moe_b_expert[0], moe_w_gate_up, moe_w_down, next_gain=norm_gain[1, 0][None, :])

    uv, ssq = _chunk_in(h, cm_w_in[0].astype(bf16))
    x = _chunk_out(uv, ssq, cm_v_gain[0][None, :], cm_w_s[0], cm_b_s[0][:, :, None], cm_w_out[0].astype(bf16),
                   x, mod, 1)
    y_ctx, y_lat = _hier_moe(x, mod, 1, norm_gain[1, 1][None, :], moe_w_group[1], moe_b_group[1], moe_w_expert[1],
                             moe_b_expert[1], moe_w_gate_up, moe_w_down)

    new_k = kv_new[:, :KV_DIM].reshape(BATCH, 1, SEQ, N_KV_HEADS, HEAD_DIM)
    new_v = kv_new[:, KV_DIM:].reshape(BATCH, 1, SEQ, N_KV_HEADS, HEAD_DIM)
    return (y_ctx.reshape(BATCH, SEQ, D_MODEL), y_lat.reshape(DEC_BATCH, DEC_SEQ, D_MODEL), new_k, new_v)
```

```python
import functools

import jax
import jax.numpy as jnp
from jax import lax
from jax.experimental import pallas as pl
from jax.experimental.pallas import tpu as pltpu

f32 = jnp.float32
bf16 = jnp.bfloat16
i32 = jnp.int32
u32 = jnp.uint32

D_MODEL = 4096
BATCH, SEQ = 16, 256
DEC_BATCH, DEC_SEQ = 4, 2048
PAST_LEN = 256
DEPTH = 2
GRID_W = 64
N_HEADS, N_KV_HEADS, HEAD_DIM = 32, 8, 128
GQA = N_HEADS // N_KV_HEADS
Q_DIM = N_HEADS * HEAD_DIM
KV_DIM = N_KV_HEADS * HEAD_DIM
QKV_DIM = Q_DIM + 2 * KV_DIM
AXIS_DIM = HEAD_DIM // 2
ROPE_THETA = 10000.0
CHUNK = 128
D_INNER = D_MODEL
N_SGU_GROUPS = 8
SGU_GROUP_DIM = D_INNER // N_SGU_GROUPS
N_EXPERT_GROUPS, EXPERTS_PER_GROUP = 4, 8
N_EXPERTS = N_EXPERT_GROUPS * EXPERTS_PER_GROUP
TOP_K = 2
D_EXPERT = D_MODEL // 4
N_MOD = 6
RMS_EPS = 1e-6

N_CTX = BATCH * SEQ
N_LAT = DEC_BATCH * DEC_SEQ
N_TOK = N_CTX + N_LAT
N_COPIES = N_TOK * TOP_K
MOD_ROWS = 8

LANES = 128
HALF_D = D_MODEL // 2

TM = 512
TM_PROJ = 1024
TN_QKV = 1024
TN_PROJ = 1024
TN_MOD = 512
TQ = 256
KEY_CHUNK = 128
TE = 256
N_ETILES = N_COPIES // TE + N_EXPERTS
R_MAX = N_ETILES * TE
TT = 512
TC = 256
ISSUE_UNROLL = 8
LOG2E = 1.4426950408889634
ROUTER_LANES = 128
EXPERT_LANE0 = 32

VMEM_LIMIT = 56 * 1024 * 1024


def _params(n_axes):
    return pltpu.CompilerParams(dimension_semantics=("arbitrary",) * n_axes, vmem_limit_bytes=VMEM_LIMIT)


def _mod_row(i, tm):
    start = i * tm
    return jnp.where(start < N_CTX, 0, 1 + (start - N_CTX) // DEC_SEQ)


def _mod_spec(layer, k, tm, tn=D_MODEL, col=None, row_fn=None):
    row_fn = row_fn or (lambda i: _mod_row(i, tm))
    if col is None:
        return pl.BlockSpec((None, None, None, 1, tn), lambda i, j: (layer, row_fn(i), k, 0, 0))
    return pl.BlockSpec((None, None, None, 1, tn), lambda i, j: (layer, row_fn(i), k, 0, j))


def _adaln(x, gain, shift, scale):
    ms = jnp.mean(x * x, axis=-1, keepdims=True)
    return x * lax.rsqrt(ms + RMS_EPS) * gain * (1.0 + scale) + shift


def _mod_kernel(c_ref, w_ref, b_ref, o_ref):
    s = jax.nn.silu(c_ref[...]).astype(bf16)
    o_ref[...] = jnp.dot(s, w_ref[...].astype(bf16), preferred_element_type=f32) + b_ref[...]


def _modulation(cond, w_mod, b_mod):
    n = N_MOD * D_MODEL
    out = pl.pallas_call(
        _mod_kernel,
        grid=(DEPTH, n // TN_MOD),
        in_specs=[
            pl.BlockSpec((MOD_ROWS, D_MODEL), lambda l, j: (0, 0)),
            pl.BlockSpec((None, D_MODEL, TN_MOD), lambda l, j: (l, 0, j)),
            pl.BlockSpec((None, 1, TN_MOD), lambda l, j: (l, 0, j)),
        ],
        out_specs=pl.BlockSpec((None, MOD_ROWS, TN_MOD), lambda l, j: (l, 0, j)),
        out_shape=jax.ShapeDtypeStruct((DEPTH, MOD_ROWS, n), f32),
        compiler_params=_params(2),
        name="modulation",
    )(cond, w_mod, b_mod.reshape(DEPTH, 1, n))
    return out.reshape(DEPTH, MOD_ROWS, N_MOD, 1, D_MODEL)


def _qkv_kernel(*refs, rope, with_cache):
    x_ref, g_ref, sh_ref, sc_ref, w_ref, hg_ref = refs[:6]
    pos = 6
    if rope:
        cos_ref, sa_ref, sb_ref = refs[pos:pos + 3]
        pos += 3
    o_ref = refs[pos]
    pos += 1
    if with_cache:
        kv_ref = refs[pos]
        pos += 1
    h_scr = refs[pos]

    j = pl.program_id(1)
    nq = Q_DIM // TN_QKV
    nk = KV_DIM // TN_QKV

    @pl.when(j == 0)
    def _():
        h_scr[...] = _adaln(x_ref[...], g_ref[...], sh_ref[...], sc_ref[...]).astype(bf16)

    acc = jnp.dot(h_scr[...], w_ref[...], preferred_element_type=f32)

    is_v = j >= nq + nk
    gain = hg_ref[...]
    for hh in range(TN_QKV // HEAD_DIM):
        cols = slice(hh * HEAD_DIM, (hh + 1) * HEAD_DIM)
        a = acc[:, cols]
        y = a * lax.rsqrt(jnp.mean(a * a, axis=-1, keepdims=True) + RMS_EPS) * gain
        if rope:
            y = (y * cos_ref[...] + pltpu.roll(y, HEAD_DIM - AXIS_DIM // 2, 1) * sa_ref[...]
                 + pltpu.roll(y, AXIS_DIM // 2, 1) * sb_ref[...])
        y = jnp.where(is_v, a, y)
        o_ref[:, cols] = y.astype(bf16)
        if with_cache:
            kv_ref[:, cols] = y


def _qkv_proj(x, mod, layer, gain, w_qkv, q_gain, k_gain, *, tok0, rope_tabs=None, with_cache=False):
    n_tok = x.shape[0]
    blk0 = tok0 // TM
    nj = QKV_DIM // TN_QKV
    nq = Q_DIM // TN_QKV
    nk = KV_DIM // TN_QKV
    head_gain = jnp.stack([q_gain * (HEAD_DIM ** -0.5 * LOG2E)] * nq + [k_gain] * nk
                          + [jnp.ones_like(k_gain)] * (nj - nq - nk))
    row_fn = lambda i: _mod_row(i + blk0, TM)
    in_specs = [
        pl.BlockSpec((TM, D_MODEL), lambda i, j: (i, 0)),
        pl.BlockSpec((1, D_MODEL), lambda i, j: (0, 0)),
        _mod_spec(layer, 0, TM, row_fn=row_fn),
        _mod_spec(layer, 1, TM, row_fn=row_fn),
        pl.BlockSpec((D_MODEL, TN_QKV), lambda i, j: (0, j)),
        pl.BlockSpec((None, 1, HEAD_DIM), lambda i, j: (j, 0, 0)),
    ]
    args = [x, gain, mod, mod, w_qkv, head_gain]
    if rope_tabs is not None:
        nb = DEC_SEQ // TM
        in_specs += [pl.BlockSpec((TM, HEAD_DIM), lambda i, j: (i % nb, 0))] * 3
        args += list(rope_tabs)
    out_specs = [pl.BlockSpec((TM, TN_QKV), lambda i, j: (i, j))]
    out_shape = [jax.ShapeDtypeStruct((n_tok, QKV_DIM), bf16)]
    if with_cache:
        out_specs.append(pl.BlockSpec((TM, TN_QKV), lambda i, j: (i, jnp.maximum(j - nq, 0))))
        out_shape.append(jax.ShapeDtypeStruct((n_tok, 2 * KV_DIM), f32))
    return pl.pallas_call(
        functools.partial(_qkv_kernel, rope=rope_tabs is not None, with_cache=with_cache),
        grid=(n_tok // TM, nj),
        in_specs=in_specs,
        out_specs=out_specs,
        out_shape=out_shape,
        scratch_shapes=[pltpu.VMEM((TM, D_MODEL), bf16)],
        compiler_params=_params(2),
        name="qkv_rope" if rope_tabs is not None else "qkv_ctx",
    )(*args)


def _rope_tables():
    t = jnp.arange(DEC_SEQ)
    row = (t // GRID_W).astype(f32)
    col = (t % GRID_W).astype(f32)
    inv_freq = ROPE_THETA ** (-jnp.arange(0, AXIS_DIM, 2, dtype=f32) / AXIS_DIM)
    lane = jnp.arange(HEAD_DIM)
    freq = inv_freq[lane % (AXIS_DIM // 2)]
    pos = jnp.where(lane[None, :] < AXIS_DIM, row[:, None], col[:, None])
    ang = pos * freq[None, :]
    first = (lane % AXIS_DIM) < (AXIS_DIM // 2)
    sin = jnp.sin(ang)
    return jnp.cos(ang), jnp.where(first[None, :], -sin, 0.0), jnp.where(first[None, :], 0.0, sin)


def _attn_kernel(*refs, with_cache):
    q_ref, k_ref, v_ref = refs[:3]
    if with_cache:
        ck_ref, cv_ref, o_ref, k_all, vt_all = refs[3:8]
    else:
        o_ref, k_all, vt_all = refs[3:6]
    n_keys = k_all.shape[0]
    kc = KEY_CHUNK

    @pl.when(pl.program_id(2) == 0)
    def _():
        off = 0
        if with_cache:
            k_all[0:PAST_LEN, :] = ck_ref[...].astype(bf16)
            for c in range(PAST_LEN // kc):
                vt_all[:, c * kc:(c + 1) * kc] = cv_ref[c * kc:(c + 1) * kc, :].T.astype(bf16)
            off = PAST_LEN
        k_all[off:n_keys, :] = k_ref[...]
        for c in range((n_keys - off) // kc):
            vt_all[:, off + c * kc:off + (c + 1) * kc] = v_ref[c * kc:(c + 1) * kc, :].astype(f32).T.astype(bf16)

    nt_dims = (((1,), (1,)), ((), ()))
    for g in range(GQA):
        qg = q_ref[:, g * HEAD_DIM:(g + 1) * HEAD_DIM]
        m = l = acc = None
        for c in range(n_keys // kc):
            st = lax.dot_general(k_all[c * kc:(c + 1) * kc, :], qg, nt_dims, preferred_element_type=f32)
            ms = jnp.max(st, axis=0, keepdims=True)
            vt = vt_all[:, c * kc:(c + 1) * kc]
            if m is None:
                m = ms
                p = jnp.exp2(st - m)
                l = jnp.sum(p, axis=0, keepdims=True)
                acc = jnp.dot(vt, p.astype(bf16), preferred_element_type=f32)
            else:
                m_new = jnp.maximum(m, ms)
                alpha = jnp.exp2(m - m_new)
                p = jnp.exp2(st - m_new)
                l = alpha * l + jnp.sum(p, axis=0, keepdims=True)
                acc = alpha * acc + jnp.dot(vt, p.astype(bf16), preferred_element_type=f32)
                m = m_new
        o_ref[:, g * HEAD_DIM:(g + 1) * HEAD_DIM] = (acc / l).T.astype(bf16)


def _attention(qkv, *, n_batch, seq, tq, cache=None):
    nqb = seq // tq
    n_keys = seq + (PAST_LEN if cache is not None else 0)
    kcol = Q_DIM // HEAD_DIM
    vcol = (Q_DIM + KV_DIM) // HEAD_DIM
    in_specs = [
        pl.BlockSpec((tq, GQA * HEAD_DIM), lambda b, h, i: (b * nqb + i, h)),
        pl.BlockSpec((seq, HEAD_DIM), lambda b, h, i: (b, kcol + h)),
        pl.BlockSpec((seq, HEAD_DIM), lambda b, h, i: (b, vcol + h)),
    ]
    args = [qkv, qkv, qkv]
    if cache is not None:
        in_specs += [pl.BlockSpec((None, PAST_LEN, HEAD_DIM), lambda b, h, i: (b, 0, h))] * 2
        args += list(cache)
    return pl.pallas_call(
        functools.partial(_attn_kernel, with_cache=cache is not None),
        grid=(n_batch, N_KV_HEADS, nqb),
        in_specs=in_specs,
        out_specs=pl.BlockSpec((tq, GQA * HEAD_DIM), lambda b, h, i: (b * nqb + i, h)),
        out_shape=jax.ShapeDtypeStruct((n_batch * seq, Q_DIM), bf16),
        scratch_shapes=[pltpu.VMEM((n_keys, HEAD_DIM), bf16), pltpu.VMEM((HEAD_DIM, n_keys), bf16)],
        compiler_params=_params(3),
        name="attn_lat" if cache is not None else "attn_ctx",
    )(*args)


def _proj_res_kernel(a_ref, w_ref, x_ref, gate_ref, *rest):
    o_ref = rest[-1]
    o_ref[...] = x_ref[...] + gate_ref[...] * jnp.dot(a_ref[...], w_ref[...], preferred_element_type=f32)


def _proj_res(a, w, x, mod, layer, *, tok0, out=None):
    n_tok = a.shape[0]
    tm = TM_PROJ
    blk0 = tok0 // tm
    row_fn = lambda i: _mod_row(i + blk0, tm)
    in_specs = [
        pl.BlockSpec((tm, a.shape[1]), lambda i, j: (i, 0)),
        pl.BlockSpec((a.shape[1], TN_PROJ), lambda i, j: (0, j)),
        pl.BlockSpec((tm, TN_PROJ), lambda i, j: (i, j)),
        _mod_spec(layer, 2, tm, tn=TN_PROJ, col=True, row_fn=row_fn),
    ]
    args = [a, w, x, mod]
    aliases = {}
    if out is not None:
        in_specs.append(pl.BlockSpec(memory_space=pl.ANY))
        args.append(out)
        aliases = {len(args) - 1: 0}
    return pl.pallas_call(
        _proj_res_kernel,
        grid=(n_tok // tm, D_MODEL // TN_PROJ),
        in_specs=in_specs,
        out_specs=pl.BlockSpec((tm, TN_PROJ), lambda i, j: (i + blk0, j)),
        out_shape=jax.ShapeDtypeStruct((N_TOK, D_MODEL), f32),
        input_output_aliases=aliases,
        compiler_params=_params(2),
        name="proj_res",
    )(*args)


def _win_kernel(h_ref, w_ref, o_ref, ssq_ref, *, tn):
    j = pl.program_id(1)
    nu = D_INNER // tn

    @pl.when(j == 0)
    def _():
        ssq_ref[...] = jnp.zeros_like(ssq_ref)

    y = jax.nn.gelu(jnp.dot(h_ref[...], w_ref[...], preferred_element_type=f32))
    o_ref[...] = y.astype(bf16)

    y2 = y * y
    part = y2[:, 0:LANES]
    for c in range(1, tn // LANES):
        part = part + y2[:, c * LANES:(c + 1) * LANES]
    ssq_ref[...] += jnp.where(j >= nu, part, 0.0)


def _chunk_in(h, w_in):
    tn = TN_QKV
    tm = TM_PROJ
    return pl.pallas_call(
        functools.partial(_win_kernel, tn=tn),
        grid=(N_TOK // tm, 2 * D_INNER // tn),
        in_specs=[
            pl.BlockSpec((tm, D_MODEL), lambda i, j: (i, 0)),
            pl.BlockSpec((D_MODEL, tn), lambda i, j: (0, j)),
        ],
        out_specs=[
            pl.BlockSpec((tm, tn), lambda i, j: (i, j)),
            pl.BlockSpec((tm, LANES), lambda i, j: (i, 0)),
        ],
        out_shape=[jax.ShapeDtypeStruct((N_TOK, 2 * D_INNER), bf16), jax.ShapeDtypeStruct((N_TOK, LANES), f32)],
        compiler_params=_params(2),
        name="chunk_in",
    )(h, w_in)


def _sgu_out_kernel(u_ref, v_ref, ssq_ref, vg_ref, ws_ref, bs_ref, w_ref, x_ref, gate_ref, o_ref, a_scr):
    j = pl.program_id(1)

    @pl.when(j == 0)
    def _():
        rstd = lax.rsqrt(jnp.sum(ssq_ref[...], axis=-1, keepdims=True) * (1.0 / D_INNER) + RMS_EPS)
        for c in range(TM // CHUNK):
            rows = slice(c * CHUNK, (c + 1) * CHUNK)
            rs = rstd[rows]
            for g in range(N_SGU_GROUPS):
                cols = slice(g * SGU_GROUP_DIM, (g + 1) * SGU_GROUP_DIM)
                vn = (v_ref[rows, cols].astype(f32) * rs * vg_ref[:, cols]).astype(bf16)
                vm = jnp.dot(ws_ref[g].astype(bf16), vn, preferred_element_type=f32) + bs_ref[g]
                a_scr[rows, cols] = (u_ref[rows, cols].astype(f32) * vm).astype(bf16)

    o_ref[...] = x_ref[...] + gate_ref[...] * jnp.dot(a_scr[...], w_ref[...], preferred_element_type=f32)


def _chunk_out(uv, ssq, v_gain, w_s, b_s, w_out, x, mod, layer):
    return pl.pallas_call(
        _sgu_out_kernel,
        grid=(N_TOK // TM, D_MODEL // TN_PROJ),
        in_specs=[
            pl.BlockSpec((TM, D_INNER), lambda i, j: (i, 0)),
            pl.BlockSpec((TM, D_INNER), lambda i, j: (i, 1)),
            pl.BlockSpec((TM, LANES), lambda i, j: (i, 0)),
            pl.BlockSpec((1, D_INNER), lambda i, j: (0, 0)),
            pl.BlockSpec((N_SGU_GROUPS, CHUNK, CHUNK), lambda i, j: (0, 0, 0)),
            pl.BlockSpec((N_SGU_GROUPS, CHUNK, 1), lambda i, j: (0, 0, 0)),
            pl.BlockSpec((D_INNER, TN_PROJ), lambda i, j: (0, j)),
            pl.BlockSpec((TM, TN_PROJ), lambda i, j: (i, j)),
            _mod_spec(layer, 2, TM, tn=TN_PROJ, col=True),
        ],
        out_specs=pl.BlockSpec((TM, TN_PROJ), lambda i, j: (i, j)),
        out_shape=jax.ShapeDtypeStruct((N_TOK, D_MODEL), f32),
        scratch_shapes=[pltpu.VMEM((TM, D_INNER), bf16)],
        compiler_params=_params(2),
        name="sgu_out",
    )(uv, uv, ssq, v_gain, w_s, b_s, w_out, x, mod)


def _router_kernel(x_ref, g_ref, sh_ref, sc_ref, whi_ref, wlo_ref, b_ref, hp_ref, r_ref):
    h = _adaln(x_ref[...], g_ref[...], sh_ref[...], sc_ref[...])
    hb = h.astype(bf16)
    hf = hb.astype(f32)
    lo = (h - hf).astype(bf16)
    logits = (jnp.dot(hb, whi_ref[...], preferred_element_type=f32)
              + (jnp.dot(lo, whi_ref[...], preferred_element_type=f32)
                 + jnp.dot(hb, wlo_ref[...], preferred_element_type=f32))) + b_ref[...]

    bits = lax.bitcast_convert_type(hf, u32)
    hp_ref[...] = (bits[:, :HALF_D] >> 16) | bits[:, HALF_D:]

    lane = lax.broadcasted_iota(i32, logits.shape, 1).astype(f32)
    neg = -jnp.inf
    big = float(ROUTER_LANES)
    lg = jnp.where(lane < N_EXPERT_GROUPS, logits, neg)
    mg = jnp.max(lg, axis=-1, keepdims=True)
    pg_top = 1.0 / jnp.sum(jnp.exp(lg - mg), axis=-1, keepdims=True)
    grp = jnp.min(jnp.where(lg == mg, lane, big), axis=-1, keepdims=True)
    lane0 = EXPERT_LANE0 + EXPERTS_PER_GROUP * grp
    le = jnp.where((lane >= lane0) & (lane < lane0 + EXPERTS_PER_GROUP), logits, neg)
    m1 = jnp.max(le, axis=-1, keepdims=True)
    i1 = jnp.min(jnp.where(le == m1, lane, big), axis=-1, keepdims=True)
    le2 = jnp.where(lane == i1, neg, le)
    m2 = jnp.max(le2, axis=-1, keepdims=True)
    i2 = jnp.min(jnp.where(le2 == m2, lane, big), axis=-1, keepdims=True)
    e2 = jnp.exp(m2 - m1)
    p1 = 1.0 / (1.0 + e2)
    p2 = e2 * p1
    r_ref[...] = jnp.where(lane == 0, i1 - EXPERT_LANE0,
                           jnp.where(lane == 1, i2 - EXPERT_LANE0,
                                     jnp.where(lane == 2, pg_top * p1,
                                               jnp.where(lane == 3, pg_top * p2, 0.0))))


def _router(x, mod, layer, gain, w_group, b_group, w_expert, b_expert):
    tm = 256
    pad0 = jnp.zeros((D_MODEL, EXPERT_LANE0 - N_EXPERT_GROUPS), f32)
    pad1 = jnp.zeros((D_MODEL, ROUTER_LANES - EXPERT_LANE0 - N_EXPERTS), f32)
    w_r = jnp.concatenate([w_group, pad0, w_expert, pad1], axis=1)
    w_hi = w_r.astype(bf16)
    w_lo = (w_r - w_hi.astype(f32)).astype(bf16)
    b_r = jnp.concatenate([b_group, pad0[0], b_expert, pad1[0]])[None, :]
    return pl.pallas_call(
        _router_kernel,
        grid=(N_TOK // tm, 1),
        in_specs=[
            pl.BlockSpec((tm, D_MODEL), lambda i, j: (i, 0)),
            pl.BlockSpec((1, D_MODEL), lambda i, j: (0, 0)),
            _mod_spec(layer, 3, tm),
            _mod_spec(layer, 4, tm),
            pl.BlockSpec((D_MODEL, ROUTER_LANES), lambda i, j: (0, 0)),
            pl.BlockSpec((D_MODEL, ROUTER_LANES), lambda i, j: (0, 0)),
            pl.BlockSpec((1, ROUTER_LANES), lambda i, j: (0, 0)),
        ],
        out_specs=[
            pl.BlockSpec((tm, HALF_D), lambda i, j: (i, 0)),
            pl.BlockSpec((tm, ROUTER_LANES), lambda i, j: (i, 0)),
        ],
        out_shape=[jax.ShapeDtypeStruct((N_TOK, HALF_D), u32), jax.ShapeDtypeStruct((N_TOK, ROUTER_LANES), f32)],
        compiler_params=_params(2),
        name="router",
    )(x, gain, mod, mod, w_hi, w_lo, b_r)


def _moe_layout(route):
    flat_e = route[:, 0:TOP_K].astype(i32).reshape(-1)
    onehot = (flat_e[:, None] == jnp.arange(N_EXPERTS, dtype=i32)[None, :]).astype(i32)
    counts = jnp.sum(onehot, axis=0)
    pcounts = ((counts + TE - 1) // TE) * TE
    pend = jnp.cumsum(pcounts)
    pstart = pend - pcounts
    n_used = pend[-1] // TE
    tiles = jnp.arange(N_ETILES, dtype=i32)
    tile_e = jnp.sum((tiles[:, None] * TE >= pend[None, :]).astype(i32), axis=1)
    last_e = jnp.max(jnp.where(tiles < n_used, tile_e, 0))
    tile_e = jnp.minimum(tile_e, last_e)
    tile_blk = jnp.minimum(tiles, n_used - 1)
    last_tile_row = jnp.where(counts > 0, pend - TE, -1)
    rank = jnp.sum(onehot * (jnp.cumsum(onehot, axis=0) - 1), axis=1)
    pos = jnp.sum(onehot * pstart[None, :], axis=1) + rank
    experts = jnp.arange(N_EXPERTS, dtype=i32)
    nonempty = (counts > 0).astype(i32)
    seq_of_e = jnp.cumsum(nonempty) - 1
    n_seq = jnp.sum(nonempty)
    seq_e = jnp.sum(jnp.where((seq_of_e[None, :] == experts[:, None]) & (nonempty[None, :] > 0), experts[None, :], 0),
                    axis=1)
    tile_onehot = (tile_e[:, None] == experts[None, :]).astype(i32)
    tile_seq = jnp.sum(tile_onehot * seq_of_e[None, :], axis=1)
    tile_nt = jnp.maximum(jnp.sum(tile_onehot * (pcounts // TE)[None, :], axis=1), 1)
    as_i32 = lambda a: a.astype(i32)
    return dict(tile_e=as_i32(tile_e), tile_blk=as_i32(tile_blk), n_used=as_i32(n_used.reshape(1)),
                last_tile_row=as_i32(last_tile_row), pos=as_i32(pos), seq_e=as_i32(seq_e),
                n_seq=as_i32(n_seq.reshape(1)), tile_seq=as_i32(tile_seq), tile_nt=as_i32(tile_nt))


def _scatter_kernel(pos_ref, ltr_ref, hp_ref, xs_hbm, zero_scr, sem, zsem):
    i = pl.program_id(0)

    @pl.when(i == 0)
    def _():
        zero_scr[...] = jnp.zeros_like(zero_scr)

        def zero_copy(e):
            row0 = pl.multiple_of(ltr_ref[e], TE)
            return pltpu.make_async_copy(zero_scr, xs_hbm.at[pl.ds(row0, TE)], zsem)

        for e in range(N_EXPERTS):
            @pl.when(ltr_ref[e] >= 0)
            def _():
                zero_copy(e).start()
        for e in range(N_EXPERTS):
            @pl.when(ltr_ref[e] >= 0)
            def _():
                zero_copy(e).wait()

    base = i * (TT * TOP_K)

    def issue(r, carry):
        for k in range(TOP_K):
            p = pos_ref[base + TOP_K * r + k]
            pltpu.make_async_copy(hp_ref.at[pl.ds(r, 1)], xs_hbm.at[pl.ds(p, 1)], sem).start()
        return carry

    lax.fori_loop(0, TT, issue, 0, unroll=ISSUE_UNROLL)
    for k in range(TOP_K):
        pltpu.make_async_copy(hp_ref, xs_hbm.at[pl.ds(0, TT)], sem).wait()


def _dispatch(hp, pos, last_tile_row):
    grid_spec = pltpu.PrefetchScalarGridSpec(
        num_scalar_prefetch=2,
        grid=(N_TOK // TT,),
        in_specs=[pl.BlockSpec((TT, HALF_D), lambda i, pos, ltr: (i, 0))],
        out_specs=pl.BlockSpec(memory_space=pl.ANY),
        scratch_shapes=[pltpu.VMEM((TE, HALF_D), u32), pltpu.SemaphoreType.DMA(()), pltpu.SemaphoreType.DMA(())],
    )
    return pl.pallas_call(
        _scatter_kernel,
        grid_spec=grid_spec,
        out_shape=jax.ShapeDtypeStruct((R_MAX, HALF_D), u32),
        compiler_params=_params(1),
        name="dispatch",
    )(pos, last_tile_row, hp)


RING_BYTES = 8 * 1024 * 1024
CHUNK_ROWS = 128


def _grouped_kernel(tb_ref, nu_ref, seq_ref, nseq_ref, tseq_ref, tnt_ref, x_ref, w_hbm, o_ref, ring, wb, sem, nxt,
                    *, layer, k_dim, swiglu):
    del tb_ref
    t = pl.program_id(0)
    n_slots = ring.shape[0]
    nc = k_dim // CHUNK_ROWS
    nc_shift = nc.bit_length() - 1
    total = nc * nseq_ref[0]

    def chunk_copy(g):
        e = seq_ref[lax.shift_right_logical(g, nc_shift)]
        row = pl.multiple_of((g & (nc - 1)) * CHUNK_ROWS, CHUNK_ROWS)
        slot = g & (n_slots - 1)
        return pltpu.make_async_copy(w_hbm.at[layer, e, pl.ds(row, CHUNK_ROWS)], ring.at[slot], sem.at[slot])

    def cast_chunk(g):
        chunk_copy(g).wait()
        buf = lax.shift_right_logical(g, nc_shift) & 1
        row = pl.multiple_of((g & (nc - 1)) * CHUNK_ROWS, CHUNK_ROWS)
        wb[buf, pl.ds(row, CHUNK_ROWS), :] = ring[g & (n_slots - 1)].astype(bf16)

        @pl.when(g + n_slots < total)
        def _():
            chunk_copy(g + n_slots).start()

    def cast_next(n):
        def body(_, carry):
            cast_chunk(nxt[0])
            nxt[0] = nxt[0] + 1
            return carry

        lax.fori_loop(0, n, body, 0)

    @pl.when(t == 0)
    def _():
        nxt[0] = 0
        for g in range(n_slots):
            chunk_copy(g).start()
        cast_next(nc)

    @pl.when(t < nu_ref[0])
    def _():
        s = tseq_ref[t]
        nt = tnt_ref[t]
        share = lax.div(nc + nt - 1, nt)
        limit = jnp.minimum((s + 2) * nc, total)

        def cast_share(n):
            cast_next(jnp.maximum(jnp.minimum(n, limit - nxt[0]), 0))

        first_half = lax.shift_right_logical(share, 1)
        cast_share(first_half)
        w = wb.at[s & 1]
        if swiglu:
            xw = x_ref[...]
            lo = lax.bitcast_convert_type(xw << 16, f32).astype(bf16)
            hi = lax.bitcast_convert_type(xw & jnp.uint32(0xFFFF0000), f32).astype(bf16)
            acc = (jnp.dot(lo, w[0:HALF_D, :], preferred_element_type=f32)
                   + jnp.dot(hi, w[HALF_D:D_MODEL, :], preferred_element_type=f32))
            o_ref[...] = (jax.nn.silu(acc[:, 0:D_EXPERT]) * acc[:, D_EXPERT:2 * D_EXPERT]).astype(o_ref.dtype)
        else:
            o_ref[...] = jnp.dot(x_ref[...], w[...], preferred_element_type=f32).astype(o_ref.dtype)
        cast_share(share - first_half)


def _grouped_matmul(x, w, layer, meta, *, swiglu, out_cols, out_dtype, name):
    k_dim, n_dim = w.shape[2], w.shape[3]
    n_slots = RING_BYTES // (CHUNK_ROWS * n_dim * 4)
    grid_spec = pltpu.PrefetchScalarGridSpec(
        num_scalar_prefetch=6,
        grid=(N_ETILES,),
        in_specs=[
            pl.BlockSpec((TE, x.shape[1]), lambda t, tb, *_: (tb[t], 0)),
            pl.BlockSpec(memory_space=pl.ANY),
        ],
        out_specs=pl.BlockSpec((TE, out_cols), lambda t, tb, *_: (tb[t], 0)),
        scratch_shapes=[
            pltpu.VMEM((n_slots, CHUNK_ROWS, n_dim), f32),
            pltpu.VMEM((2, k_dim, n_dim), bf16),
            pltpu.SemaphoreType.DMA((n_slots,)),
            pltpu.SMEM((1,), i32),
        ],
    )
    return pl.pallas_call(
        functools.partial(_grouped_kernel, layer=layer, k_dim=k_dim, swiglu=swiglu),
        grid_spec=grid_spec,
        out_shape=jax.ShapeDtypeStruct((R_MAX, out_cols), out_dtype),
        compiler_params=_params(1),
        name=name,
    )(meta["tile_blk"], meta["n_used"], meta["seq_e"], meta["n_seq"], meta["tile_seq"], meta["tile_nt"], x, w)


def _gate_up(xs, w_gate_up, layer, meta):
    return _grouped_matmul(xs, w_gate_up, layer, meta, swiglu=True, out_cols=D_EXPERT, out_dtype=bf16,
                           name="moe_gate_up")


def _down(h1, w_down, layer, meta):
    return _grouped_matmul(h1, w_down, layer, meta, swiglu=False, out_cols=D_MODEL, out_dtype=f32, name="moe_down")


def _combine_kernel(pos_ref, x_ref, gate_ref, r_ref, ys_hbm, *rest, tok0, n_steps, next_norm):
    if next_norm:
        ng_ref, nsh_ref, nsc_ref, o_ref, h_ref, buf, sem = rest
    else:
        o_ref, buf, sem = rest
    i = pl.program_id(0)

    def issue(step, slot):
        base = (step * TC + tok0) * TOP_K

        def body(r, carry):
            for k in range(TOP_K):
                p = pos_ref[base + TOP_K * r + k]
                pltpu.make_async_copy(ys_hbm.at[pl.ds(p, 1)], buf.at[slot, k, pl.ds(r, 1)], sem.at[slot]).start()
            return carry

        lax.fori_loop(0, TC, body, 0, unroll=ISSUE_UNROLL)

    @pl.when(i == 0)
    def _():
        issue(0, 0)

    @pl.when(i + 1 < n_steps)
    def _():
        issue(i + 1, (i + 1) & 1)

    slot = i & 1
    for k in range(TOP_K):
        pltpu.make_async_copy(ys_hbm.at[pl.ds(0, TC)], buf.at[slot, k], sem.at[slot]).wait()
    r = r_ref[...]
    f = r[:, 2:3] * buf[slot, 0] + r[:, 3:4] * buf[slot, 1]
    y = x_ref[...] + gate_ref[...] * f
    o_ref[...] = y
    if next_norm:
        h_ref[...] = _adaln(y, ng_ref[...], nsh_ref[...], nsc_ref[...]).astype(bf16)


def _combine(x, mod, layer, route, ys, pos, *, tok0, n_tok, next_gain=None):
    blk0 = tok0 // TC
    mod_spec = lambda lyr, k: pl.BlockSpec((None, None, None, 1, D_MODEL),
                                           lambda i, pos: (lyr, _mod_row(i + blk0, TC), k, 0, 0))
    in_specs = [
        pl.BlockSpec((TC, D_MODEL), lambda i, pos: (i + blk0, 0)),
        mod_spec(layer, 5),
        pl.BlockSpec((TC, ROUTER_LANES), lambda i, pos: (i + blk0, 0)),
        pl.BlockSpec(memory_space=pl.ANY),
    ]
    args = [pos, x, mod, route, ys]
    out_specs = [pl.BlockSpec((TC, D_MODEL), lambda i, pos: (i, 0))]
    out_shape = [jax.ShapeDtypeStruct((n_tok, D_MODEL), f32)]
    if next_gain is not None:
        in_specs += [pl.BlockSpec((1, D_MODEL), lambda i, pos: (0, 0)), mod_spec(layer + 1, 0), mod_spec(layer + 1, 1)]
        args += [next_gain, mod, mod]
        out_specs.append(pl.BlockSpec((TC, D_MODEL), lambda i, pos: (i, 0)))
        out_shape.append(jax.ShapeDtypeStruct((n_tok, D_MODEL), bf16))
    grid_spec = pltpu.PrefetchScalarGridSpec(
        num_scalar_prefetch=1,
        grid=(n_tok // TC,),
        in_specs=in_specs,
        out_specs=out_specs,
        scratch_shapes=[pltpu.VMEM((2, TOP_K, TC, D_MODEL), f32), pltpu.SemaphoreType.DMA((2,))],
    )
    out = pl.pallas_call(
        functools.partial(_combine_kernel, tok0=tok0, n_steps=n_tok // TC, next_norm=next_gain is not None),
        grid_spec=grid_spec,
        out_shape=out_shape,
        compiler_params=_params(1),
        name="combine",
    )(*args)
    return out if next_gain is not None else out[0]


def _hier_moe(x, mod, layer, gain, w_group, b_group, w_expert, b_expert, w_gate_up, w_down, *, next_gain=None):
    hp, route = _router(x, mod, layer, gain, w_group, b_group, w_expert, b_expert)
    meta = _moe_layout(route)
    pos = meta["pos"]
    xs = _dispatch(hp, pos, meta["last_tile_row"])
    h1 = _gate_up(xs, w_gate_up, layer, meta)
    ys = _down(h1, w_down, layer, meta)
    if next_gain is None:
        return (_combine(x, mod, layer, route, ys, pos, tok0=0, n_tok=N_CTX),
                _combine(x, mod, layer, route, ys, pos, tok0=N_CTX, n_tok=N_LAT))
    return _combine(x, mod, layer, route, ys, pos, tok0=0, n_tok=N_TOK, next_gain=next_gain)


def kernel(x_prompt, x_sample, cache_k, cache_v, c, c_ctx, norm_gain, w_mod, b_mod, attn_w_qkv, attn_q_gain, attn_k_gain, attn_w_o, cm_w_in, cm_v_gain, cm_w_s, cm_b_s, cm_w_out, moe_w_group, moe_b_group, moe_w_expert, moe_b_expert, moe_w_gate_up, moe_w_down):
    x_ctx = x_prompt.reshape(N_CTX, D_MODEL)
    x_lat = x_sample.reshape(N_LAT, D_MODEL)
    cond =jnp.concatenate([c_ctx[None, :], c, jnp.zeros((MOD_ROWS - 1 - DEC_BATCH, D_MODEL), f32)], axis=0)
    mod = _modulation(cond, w_mod, b_mod)

    w_qkv = attn_w_qkv[0].astype(bf16)
    q_gain = attn_q_gain[0][None, :]
    k_gain = attn_k_gain[0][None, :]
    gain0 = norm_gain[0, 0][None, :]
    qkv_ctx, kv_new = _qkv_proj(x_ctx, mod, 0, gain0, w_qkv, q_gain, k_gain, tok0=0, with_cache=True)
    (qkv_lat,) = _qkv_proj(x_lat, mod, 0, gain0, w_qkv, q_gain, k_gain, tok0=N_CTX, rope_tabs=_rope_tables())
    o_ctx = _attention(qkv_ctx, n_batch=BATCH, seq=SEQ, tq=SEQ)
    ck = cache_k[:, 0].reshape(DEC_BATCH, PAST_LEN, KV_DIM)
    cv = cache_v[:, 0].reshape(DEC_BATCH, PAST_LEN, KV_DIM)
    o_lat = _attention(qkv_lat, n_batch=DEC_BATCH, seq=DEC_SEQ, tq=TQ, cache=(ck, cv))
    w_o = attn_w_o[0].astype(bf16)
    x = _proj_res(o_ctx, w_o, x_ctx, mod, 0, tok0=0)
    x = _proj_res(o_lat, w_o, x_lat, mod, 0, tok0=N_CTX, out=x)
    x, h = _hier_moe(x, mod, 0, norm_gain[0, 1][None, :], moe_w_group[0], moe_b_group[0], moe_w_expert[0],
                     moe_b_expert[0], moe_w_gate_up, moe_w_down, next_gain=norm_gain[1, 0][None, :])

    uv, ssq = _chunk_in(h, cm_w_in[0].astype(bf16))
    x = _chunk_out(uv, ssq, cm_v_gain[0][None, :], cm_w_s[0], cm_b_s[0][:, :, None], cm_w_out[0].astype(bf16),
                   x, mod, 1)
    y_ctx, y_lat = _hier_moe(x, mod, 1, norm_gain[1, 1][None, :], moe_w_group[1], moe_b_group[1], moe_w_expert[1],
                             moe_b_expert[1], moe_w_gate_up, moe_w_down)

    new_k = kv_new[:, :KV_DIM].reshape(BATCH, 1, SEQ, N_KV_HEADS, HEAD_DIM)
    new_v = kv_new[:, KV_DIM:].reshape(BATCH, 1, SEQ, N_KV_HEADS, HEAD_DIM)
    return (y_ctx.reshape(BATCH, SEQ, D_MODEL), y_lat.reshape(DEC_BATCH, DEC_SEQ, D_MODEL), new_k, new_v)
```

```python
import functools

import jax
import jax.numpy as jnp
from jax import lax
from jax.experimental import pallas as pl
from jax.experimental.pallas import tpu as pltpu

f32 = jnp.float32
bf16 = jnp.bfloat16
i32 = jnp.int32
u32 = jnp.uint32

D_MODEL = 4096
BATCH, SEQ = 16, 256
DEC_BATCH, DEC_SEQ = 4, 2048
PAST_LEN = 256
DEPTH = 2
GRID_W = 64
N_HEADS, N_KV_HEADS, HEAD_DIM = 32, 8, 128
GQA = N_HEADS // N_KV_HEADS
Q_DIM = N_HEADS * HEAD_DIM
KV_DIM = N_KV_HEADS * HEAD_DIM
QKV_DIM = Q_DIM + 2 * KV_DIM
AXIS_DIM = HEAD_DIM // 2
ROPE_THETA = 10000.0
CHUNK = 128
D_INNER = D_MODEL
N_SGU_GROUPS = 8
SGU_GROUP_DIM = D_INNER // N_SGU_GROUPS
N_EXPERT_GROUPS, EXPERTS_PER_GROUP = 4, 8
N_EXPERTS = N_EXPERT_GROUPS * EXPERTS_PER_GROUP
TOP_K = 2
D_EXPERT = D_MODEL // 4
N_MOD = 6
RMS_EPS = 1e-6

N_CTX = BATCH * SEQ
N_LAT = DEC_BATCH * DEC_SEQ
N_TOK = N_CTX + N_LAT
N_COPIES = N_TOK * TOP_K
MOD_ROWS = 8

LANES = 128
HALF_D = D_MODEL // 2

TM = 512
TM_PROJ = 1024
TN_QKV = 1024
TN_PROJ = 1024
TN_MOD = 512
TQ = 256
KEY_CHUNK = 128
TE = 256
N_ETILES = N_COPIES // TE + N_EXPERTS
R_MAX = N_ETILES * TE
CTX_KV_HEADS_PER_STEP = 4
TT = 1024
TC = 256
ISSUE_UNROLL = 8
LOG2E = 1.4426950408889634
ROUTER_LANES = 128
EXPERT_LANE0 = 32

VMEM_LIMIT = 56 * 1024 * 1024


def _params(n_axes):
    return pltpu.CompilerParams(dimension_semantics=("arbitrary",) * n_axes, vmem_limit_bytes=VMEM_LIMIT)


def _mod_row(i, tm):
    start = i * tm
    return jnp.where(start < N_CTX, 0, 1 + (start - N_CTX) // DEC_SEQ)


def _mod_spec(layer, k, tm, tn=D_MODEL, col=None, row_fn=None):
    row_fn = row_fn or (lambda i: _mod_row(i, tm))
    if col is None:
        return pl.BlockSpec((None, None, None, 1, tn), lambda i, j: (layer, row_fn(i), k, 0, 0))
    return pl.BlockSpec((None, None, None, 1, tn), lambda i, j: (layer, row_fn(i), k, 0, j))


def _adaln(x, gain, shift, scale):
    ms = jnp.mean(x * x, axis=-1, keepdims=True)
    return x * lax.rsqrt(ms + RMS_EPS) * gain * (1.0 + scale) + shift


def _mod_kernel(c_ref, w_ref, b_ref, o_ref):
    s = jax.nn.silu(c_ref[...]).astype(bf16)
    o_ref[...] = jnp.dot(s, w_ref[...].astype(bf16), preferred_element_type=f32) + b_ref[...]


def _modulation(cond, w_mod, b_mod):
    n = N_MOD * D_MODEL
    out = pl.pallas_call(
        _mod_kernel,
        grid=(DEPTH, n // TN_MOD),
        in_specs=[
            pl.BlockSpec((MOD_ROWS, D_MODEL), lambda l, j: (0, 0)),
            pl.BlockSpec((None, D_MODEL, TN_MOD), lambda l, j: (l, 0, j)),
            pl.BlockSpec((None, 1, TN_MOD), lambda l, j: (l, 0, j)),
        ],
        out_specs=pl.BlockSpec((None, MOD_ROWS, TN_MOD), lambda l, j: (l, 0, j)),
        out_shape=jax.ShapeDtypeStruct((DEPTH, MOD_ROWS, n), f32),
        compiler_params=_params(2),
        name="modulation",
    )(cond, w_mod, b_mod.reshape(DEPTH, 1, n))
    return out.reshape(DEPTH, MOD_ROWS, N_MOD, 1, D_MODEL)


def _qkv_kernel(*refs, rope, with_cache):
    x_ref, g_ref, sh_ref, sc_ref, w_ref, hg_ref = refs[:6]
    pos = 6
    if rope:
        cos_ref, sa_ref, sb_ref = refs[pos:pos + 3]
        pos += 3
    o_ref = refs[pos]
    pos += 1
    if with_cache:
        kv_ref = refs[pos]
        pos += 1
    h_scr = refs[pos]

    j = pl.program_id(1)
    nq = Q_DIM // TN_QKV
    nk = KV_DIM // TN_QKV

    @pl.when(j == 0)
    def _():
        h_scr[...] = _adaln(x_ref[...], g_ref[...], sh_ref[...], sc_ref[...]).astype(bf16)

    acc = jnp.dot(h_scr[...], w_ref[...], preferred_element_type=f32)

    is_v = j >= nq + nk
    gain = hg_ref[...]
    for hh in range(TN_QKV // HEAD_DIM):
        cols = slice(hh * HEAD_DIM, (hh + 1) * HEAD_DIM)
        a = acc[:, cols]
        y = a * lax.rsqrt(jnp.mean(a * a, axis=-1, keepdims=True) + RMS_EPS) * gain
        if rope:
            y = (y * cos_ref[...] + pltpu.roll(y, HEAD_DIM - AXIS_DIM // 2, 1) * sa_ref[...]
                 + pltpu.roll(y, AXIS_DIM // 2, 1) * sb_ref[...])
        y = jnp.where(is_v, a, y)
        o_ref[:, cols] = y.astype(bf16)
        if with_cache:
            kv_ref[:, cols] = y


def _qkv_proj(x, mod, layer, gain, w_qkv, q_gain, k_gain, *, tok0, rope_tabs=None, with_cache=False):
    n_tok = x.shape[0]
    blk0 = tok0 // TM
    nj = QKV_DIM // TN_QKV
    nq = Q_DIM // TN_QKV
    nk = KV_DIM // TN_QKV
    head_gain = jnp.stack([q_gain * (HEAD_DIM ** -0.5 * LOG2E)] * nq + [k_gain] * nk
                          + [jnp.ones_like(k_gain)] * (nj - nq - nk))
    row_fn = lambda i: _mod_row(i + blk0, TM)
    in_specs = [
        pl.BlockSpec((TM, D_MODEL), lambda i, j: (i, 0)),
        pl.BlockSpec((1, D_MODEL), lambda i, j: (0, 0)),
        _mod_spec(layer, 0, TM, row_fn=row_fn),
        _mod_spec(layer, 1, TM, row_fn=row_fn),
        pl.BlockSpec((D_MODEL, TN_QKV), lambda i, j: (0, j)),
        pl.BlockSpec((None, 1, HEAD_DIM), lambda i, j: (j, 0, 0)),
    ]
    args = [x, gain, mod, mod, w_qkv, head_gain]
    if rope_tabs is not None:
        nb = DEC_SEQ // TM
        in_specs += [pl.BlockSpec((TM, HEAD_DIM), lambda i, j: (i % nb, 0))] * 3
        args += list(rope_tabs)
    out_specs = [pl.BlockSpec((TM, TN_QKV), lambda i, j: (i, j))]
    out_shape = [jax.ShapeDtypeStruct((n_tok, QKV_DIM), bf16)]
    if with_cache:
        out_specs.append(pl.BlockSpec((TM, TN_QKV), lambda i, j: (i, jnp.maximum(j - nq, 0))))
        out_shape.append(jax.ShapeDtypeStruct((n_tok, 2 * KV_DIM), f32))
    return pl.pallas_call(
        functools.partial(_qkv_kernel, rope=rope_tabs is not None, with_cache=with_cache),
        grid=(n_tok // TM, nj),
        in_specs=in_specs,
        out_specs=out_specs,
        out_shape=out_shape,
        scratch_shapes=[pltpu.VMEM((TM, D_MODEL), bf16)],
        compiler_params=_params(2),
        name="qkv_rope" if rope_tabs is not None else "qkv_ctx",
    )(*args)


def _rope_tables():
    t = jnp.arange(DEC_SEQ)
    row = (t // GRID_W).astype(f32)
    col = (t % GRID_W).astype(f32)
    inv_freq = ROPE_THETA ** (-jnp.arange(0, AXIS_DIM, 2, dtype=f32) / AXIS_DIM)
    lane = jnp.arange(HEAD_DIM)
    freq = inv_freq[lane % (AXIS_DIM // 2)]
    pos = jnp.where(lane[None, :] < AXIS_DIM, row[:, None], col[:, None])
    ang = pos * freq[None, :]
    first = (lane % AXIS_DIM) < (AXIS_DIM // 2)
    sin = jnp.sin(ang)
    return jnp.cos(ang), jnp.where(first[None, :], -sin, 0.0), jnp.where(first[None, :], 0.0, sin)


def _attn_kernel(*refs, with_cache):
    q_ref, k_ref, v_ref = refs[:3]
    if with_cache:
        ck_ref, cv_ref, o_ref, k_all, vt_all = refs[3:8]
    else:
        o_ref, k_all, vt_all = refs[3:6]
    n_keys = k_all.shape[0]
    kv_heads = k_ref.shape[1] // HEAD_DIM
    kc = KEY_CHUNK

    @pl.when(pl.program_id(2) == 0)
    def _():
        off = 0
        if with_cache:
            k_all[0:PAST_LEN, :] = ck_ref[...].astype(bf16)
            for c in range(PAST_LEN // kc):
                vt_all[:, c * kc:(c + 1) * kc] = cv_ref[c * kc:(c + 1) * kc, :].T.astype(bf16)
            off = PAST_LEN
        k_all[off:n_keys, :] = k_ref[...]
        for kv in range(kv_heads):
            cols = slice(kv * HEAD_DIM, (kv + 1) * HEAD_DIM)
            for c in range((n_keys - off) // kc):
                vt_all[cols, off + c * kc:off + (c + 1) * kc] = (
                    v_ref[c * kc:(c + 1) * kc, cols].astype(f32).T.astype(bf16))

    nt_dims = (((1,), (1,)), ((), ()))
    for head in range(kv_heads * GQA):
        g = head
        kv_cols = slice((head // GQA) * HEAD_DIM, (head // GQA + 1) * HEAD_DIM)
        qg = q_ref[:, g * HEAD_DIM:(g + 1) * HEAD_DIM]
        m = l = acc = None
        for c in range(n_keys // kc):
            st = lax.dot_general(k_all[c * kc:(c + 1) * kc, kv_cols], qg, nt_dims, preferred_element_type=f32)
            ms = jnp.max(st, axis=0, keepdims=True)
            vt = vt_all[kv_cols, c * kc:(c + 1) * kc]
            if m is None:
                m = ms
                p = jnp.exp2(st - m)
                l = jnp.sum(p, axis=0, keepdims=True)
                acc = jnp.dot(vt, p.astype(bf16), preferred_element_type=f32)
            else:
                m_new = jnp.maximum(m, ms)
                alpha = jnp.exp2(m - m_new)
                p = jnp.exp2(st - m_new)
                l = alpha * l + jnp.sum(p, axis=0, keepdims=True)
                acc = alpha * acc + jnp.dot(vt, p.astype(bf16), preferred_element_type=f32)
                m = m_new
        o_ref[:, g * HEAD_DIM:(g + 1) * HEAD_DIM] = (acc / l).T.astype(bf16)


def _attention(qkv, *, n_batch, seq, tq, cache=None):
    nqb = seq // tq
    n_keys = seq + (PAST_LEN if cache is not None else 0)
    kvb = 1 if cache is not None else CTX_KV_HEADS_PER_STEP
    kv_w = kvb * HEAD_DIM
    kcol = Q_DIM // kv_w
    vcol = (Q_DIM + KV_DIM) // kv_w
    in_specs = [
        pl.BlockSpec((tq, GQA * kv_w), lambda b, h, i: (b * nqb + i, h)),
        pl.BlockSpec((seq, kv_w), lambda b, h, i: (b, kcol + h)),
        pl.BlockSpec((seq, kv_w), lambda b, h, i: (b, vcol + h)),
    ]
    args = [qkv, qkv, qkv]
    if cache is not None:
        in_specs += [pl.BlockSpec((None, PAST_LEN, HEAD_DIM), lambda b, h, i: (b, 0, h))] * 2
        args += list(cache)
    return pl.pallas_call(
        functools.partial(_attn_kernel, with_cache=cache is not None),
        grid=(n_batch, N_KV_HEADS // kvb, nqb),
        in_specs=in_specs,
        out_specs=pl.BlockSpec((tq, GQA * kv_w), lambda b, h, i: (b * nqb + i, h)),
        out_shape=jax.ShapeDtypeStruct((n_batch * seq, Q_DIM), bf16),
        scratch_shapes=[pltpu.VMEM((n_keys, kv_w), bf16), pltpu.VMEM((kv_w, n_keys), bf16)],
        compiler_params=_params(3),
        name="attn_lat" if cache is not None else "attn_ctx",
    )(*args)


def _proj_res_kernel(a_ref, w_ref, x_ref, gate_ref, *rest):
    o_ref = rest[-1]
    o_ref[...] = x_ref[...] + gate_ref[...] * jnp.dot(a_ref[...], w_ref[...], preferred_element_type=f32)


def _proj_res(a, w, x, mod, layer, *, tok0, out=None):
    n_tok = a.shape[0]
    tm = TM_PROJ
    blk0 = tok0 // tm
    row_fn = lambda i: _mod_row(i + blk0, tm)
    in_specs = [
        pl.BlockSpec((tm, a.shape[1]), lambda i, j: (i, 0)),
        pl.BlockSpec((a.shape[1], TN_PROJ), lambda i, j: (0, j)),
        pl.BlockSpec((tm, TN_PROJ), lambda i, j: (i, j)),
        _mod_spec(layer, 2, tm, tn=TN_PROJ, col=True, row_fn=row_fn),
    ]
    args = [a, w, x, mod]
    aliases = {}
    if out is not None:
        in_specs.append(pl.BlockSpec(memory_space=pl.ANY))
        args.append(out)
        aliases = {len(args) - 1: 0}
    return pl.pallas_call(
        _proj_res_kernel,
        grid=(n_tok // tm, D_MODEL // TN_PROJ),
        in_specs=in_specs,
        out_specs=pl.BlockSpec((tm, TN_PROJ), lambda i, j: (i + blk0, j)),
        out_shape=jax.ShapeDtypeStruct((N_TOK, D_MODEL), f32),
        input_output_aliases=aliases,
        compiler_params=_params(2),
        name="proj_res",
    )(*args)


def _win_kernel(h_ref, w_ref, o_ref, ssq_ref, *, tn):
    j = pl.program_id(1)
    nu = D_INNER // tn

    @pl.when(j == 0)
    def _():
        ssq_ref[...] = jnp.zeros_like(ssq_ref)

    y = jax.nn.gelu(jnp.dot(h_ref[...], w_ref[...], preferred_element_type=f32))
    o_ref[...] = y.astype(bf16)

    y2 = y * y
    part = y2[:, 0:LANES]
    for c in range(1, tn // LANES):
        part = part + y2[:, c * LANES:(c + 1) * LANES]
    ssq_ref[...] += jnp.where(j >= nu, part, 0.0)


def _chunk_in(h, w_in):
    tn = TN_QKV
    tm = TM_PROJ
    return pl.pallas_call(
        functools.partial(_win_kernel, tn=tn),
        grid=(N_TOK // tm, 2 * D_INNER // tn),
        in_specs=[
            pl.BlockSpec((tm, D_MODEL), lambda i, j: (i, 0)),
            pl.BlockSpec((D_MODEL, tn), lambda i, j: (0, j)),
        ],
        out_specs=[
            pl.BlockSpec((tm, tn), lambda i, j: (i, j)),
            pl.BlockSpec((tm, LANES), lambda i, j: (i, 0)),
        ],
        out_shape=[jax.ShapeDtypeStruct((N_TOK, 2 * D_INNER), bf16), jax.ShapeDtypeStruct((N_TOK, LANES), f32)],
        compiler_params=_params(2),
        name="chunk_in",
    )(h, w_in)


def _sgu_out_kernel(u_ref, v_ref, ssq_ref, vg_ref, ws_ref, bs_ref, w_ref, x_ref, gate_ref, o_ref, a_scr):
    j = pl.program_id(1)

    @pl.when(j == 0)
    def _():
        rstd = lax.rsqrt(jnp.sum(ssq_ref[...], axis=-1, keepdims=True) * (1.0 / D_INNER) + RMS_EPS)
        for c in range(TM // CHUNK):
            rows = slice(c * CHUNK, (c + 1) * CHUNK)
            rs = rstd[rows]
            for g in range(N_SGU_GROUPS):
                cols = slice(g * SGU_GROUP_DIM, (g + 1) * SGU_GROUP_DIM)
                vn = (v_ref[rows, cols].astype(f32) * rs * vg_ref[:, cols]).astype(bf16)
                vm = jnp.dot(ws_ref[g].astype(bf16), vn, preferred_element_type=f32) + bs_ref[g]
                a_scr[rows, cols] = (u_ref[rows, cols].astype(f32) * vm).astype(bf16)

    o_ref[...] = x_ref[...] + gate_ref[...] * jnp.dot(a_scr[...], w_ref[...], preferred_element_type=f32)


def _chunk_out(uv, ssq, v_gain, w_s, b_s, w_out, x, mod, layer):
    return pl.pallas_call(
        _sgu_out_kernel,
        grid=(N_TOK // TM, D_MODEL // TN_PROJ),
        in_specs=[
            pl.BlockSpec((TM, D_INNER), lambda i, j: (i, 0)),
            pl.BlockSpec((TM, D_INNER), lambda i, j: (i, 1)),
            pl.BlockSpec((TM, LANES), lambda i, j: (i, 0)),
            pl.BlockSpec((1, D_INNER), lambda i, j: (0, 0)),
            pl.BlockSpec((N_SGU_GROUPS, CHUNK, CHUNK), lambda i, j: (0, 0, 0)),
            pl.BlockSpec((N_SGU_GROUPS, CHUNK, 1), lambda i, j: (0, 0, 0)),
            pl.BlockSpec((D_INNER, TN_PROJ), lambda i, j: (0, j)),
            pl.BlockSpec((TM, TN_PROJ), lambda i, j: (i, j)),
            _mod_spec(layer, 2, TM, tn=TN_PROJ, col=True),
        ],
        out_specs=pl.BlockSpec((TM, TN_PROJ), lambda i, j: (i, j)),
        out_shape=jax.ShapeDtypeStruct((N_TOK, D_MODEL), f32),
        scratch_shapes=[pltpu.VMEM((TM, D_INNER), bf16)],
        compiler_params=_params(2),
        name="sgu_out",
    )(uv, uv, ssq, v_gain, w_s, b_s, w_out, x, mod)


def _router_kernel(x_ref, g_ref, sh_ref, sc_ref, whi_ref, wlo_ref, b_ref, hp_ref, r_ref):
    h = _adaln(x_ref[...], g_ref[...], sh_ref[...], sc_ref[...])
    hb = h.astype(bf16)
    hf = hb.astype(f32)
    lo = (h - hf).astype(bf16)
    logits = (jnp.dot(hb, whi_ref[...], preferred_element_type=f32)
              + (jnp.dot(lo, whi_ref[...], preferred_element_type=f32)
                 + jnp.dot(hb, wlo_ref[...], preferred_element_type=f32))) + b_ref[...]

    bits = lax.bitcast_convert_type(hf, u32)
    hp_ref[...] = (bits[:, :HALF_D] >> 16) | bits[:, HALF_D:]

    lane = lax.broadcasted_iota(i32, logits.shape, 1).astype(f32)
    neg = -jnp.inf
    big = float(ROUTER_LANES)
    lg = jnp.where(lane < N_EXPERT_GROUPS, logits, neg)
    mg = jnp.max(lg, axis=-1, keepdims=True)
    pg_top = 1.0 / jnp.sum(jnp.exp(lg - mg), axis=-1, keepdims=True)
    grp = jnp.min(jnp.where(lg == mg, lane, big), axis=-1, keepdims=True)
    lane0 = EXPERT_LANE0 + EXPERTS_PER_GROUP * grp
    le = jnp.where((lane >= lane0) & (lane < lane0 + EXPERTS_PER_GROUP), logits, neg)
    m1 = jnp.max(le, axis=-1, keepdims=True)
    i1 = jnp.min(jnp.where(le == m1, lane, big), axis=-1, keepdims=True)
    le2 = jnp.where(lane == i1, neg, le)
    m2 = jnp.max(le2, axis=-1, keepdims=True)
    i2 = jnp.min(jnp.where(le2 == m2, lane, big), axis=-1, keepdims=True)
    e2 = jnp.exp(m2 - m1)
    p1 = 1.0 / (1.0 + e2)
    p2 = e2 * p1
    r_ref[...] = jnp.where(lane == 0, i1 - EXPERT_LANE0,
                           jnp.where(lane == 1, i2 - EXPERT_LANE0,
                                     jnp.where(lane == 2, pg_top * p1,
                                               jnp.where(lane == 3, pg_top * p2, 0.0))))


def _router(x, mod, layer, gain, w_group, b_group, w_expert, b_expert):
    tm = 256
    pad0 = jnp.zeros((D_MODEL, EXPERT_LANE0 - N_EXPERT_GROUPS), f32)
    pad1 = jnp.zeros((D_MODEL, ROUTER_LANES - EXPERT_LANE0 - N_EXPERTS), f32)
    w_r = jnp.concatenate([w_group, pad0, w_expert, pad1], axis=1)
    w_hi = w_r.astype(bf16)
    w_lo = (w_r - w_hi.astype(f32)).astype(bf16)
    b_r = jnp.concatenate([b_group, pad0[0], b_expert, pad1[0]])[None, :]
    return pl.pallas_call(
        _router_kernel,
        grid=(N_TOK // tm, 1),
        in_specs=[
            pl.BlockSpec((tm, D_MODEL), lambda i, j: (i, 0)),
            pl.BlockSpec((1, D_MODEL), lambda i, j: (0, 0)),
            _mod_spec(layer, 3, tm),
            _mod_spec(layer, 4, tm),
            pl.BlockSpec((D_MODEL, ROUTER_LANES), lambda i, j: (0, 0)),
            pl.BlockSpec((D_MODEL, ROUTER_LANES), lambda i, j: (0, 0)),
            pl.BlockSpec((1, ROUTER_LANES), lambda i, j: (0, 0)),
        ],
        out_specs=[
            pl.BlockSpec((tm, HALF_D), lambda i, j: (i, 0)),
            pl.BlockSpec((tm, ROUTER_LANES), lambda i, j: (i, 0)),
        ],
        out_shape=[jax.ShapeDtypeStruct((N_TOK, HALF_D), u32), jax.ShapeDtypeStruct((N_TOK, ROUTER_LANES), f32)],
        compiler_params=_params(2),
        name="router",
    )(x, gain, mod, mod, w_hi, w_lo, b_r)


def _moe_layout(route):
    flat_e = route[:, 0:TOP_K].astype(i32).reshape(-1)
    onehot = (flat_e[:, None] == jnp.arange(N_EXPERTS, dtype=i32)[None, :]).astype(i32)
    counts = jnp.sum(onehot, axis=0)
    pcounts = ((counts + TE - 1) // TE) * TE
    pend = jnp.cumsum(pcounts)
    pstart = pend - pcounts
    n_used = pend[-1] // TE
    tiles = jnp.arange(N_ETILES, dtype=i32)
    tile_e = jnp.sum((tiles[:, None] * TE >= pend[None, :]).astype(i32), axis=1)
    last_e = jnp.max(jnp.where(tiles < n_used, tile_e, 0))
    tile_e = jnp.minimum(tile_e, last_e)
    tile_blk = jnp.minimum(tiles, n_used - 1)
    last_tile_row = jnp.where(counts > 0, pend - TE, -1)
    rank = jnp.sum(onehot * (jnp.cumsum(onehot, axis=0) - 1), axis=1)
    pos = jnp.sum(onehot * pstart[None, :], axis=1) + rank
    experts = jnp.arange(N_EXPERTS, dtype=i32)
    nonempty = (counts > 0).astype(i32)
    seq_of_e = jnp.cumsum(nonempty) - 1
    n_seq = jnp.sum(nonempty)
    seq_e = jnp.sum(jnp.where((seq_of_e[None, :] == experts[:, None]) & (nonempty[None, :] > 0), experts[None, :], 0),
                    axis=1)
    tile_onehot = (tile_e[:, None] == experts[None, :]).astype(i32)
    tile_seq = jnp.sum(tile_onehot * seq_of_e[None, :], axis=1)
    tile_nt = jnp.maximum(jnp.sum(tile_onehot * (pcounts // TE)[None, :], axis=1), 1)
    as_i32 = lambda a: a.astype(i32)
    return dict(tile_e=as_i32(tile_e), tile_blk=as_i32(tile_blk), n_used=as_i32(n_used.reshape(1)),
                last_tile_row=as_i32(last_tile_row), pos=as_i32(pos), seq_e=as_i32(seq_e),
                n_seq=as_i32(n_seq.reshape(1)), tile_seq=as_i32(tile_seq), tile_nt=as_i32(tile_nt))


def _scatter_kernel(pos_ref, ltr_ref, hp_ref, xs_hbm, zero_scr, sem, zsem):
    i = pl.program_id(0)

    @pl.when(i == 0)
    def _():
        zero_scr[...] = jnp.zeros_like(zero_scr)

        def zero_copy(e):
            row0 = pl.multiple_of(ltr_ref[e], TE)
            return pltpu.make_async_copy(zero_scr, xs_hbm.at[pl.ds(row0, TE)], zsem)

        for e in range(N_EXPERTS):
            @pl.when(ltr_ref[e] >= 0)
            def _():
                zero_copy(e).start()
        for e in range(N_EXPERTS):
            @pl.when(ltr_ref[e] >= 0)
            def _():
                zero_copy(e).wait()

    base = i * (TT * TOP_K)

    def issue(r, carry):
        for k in range(TOP_K):
            p = pos_ref[base + TOP_K * r + k]
            pltpu.make_async_copy(hp_ref.at[pl.ds(r, 1)], xs_hbm.at[pl.ds(p, 1)], sem).start()
        return carry

    lax.fori_loop(0, TT, issue, 0, unroll=ISSUE_UNROLL)
    for k in range(TOP_K):
        pltpu.make_async_copy(hp_ref, xs_hbm.at[pl.ds(0, TT)], sem).wait()


def _dispatch(hp, pos, last_tile_row):
    grid_spec = pltpu.PrefetchScalarGridSpec(
        num_scalar_prefetch=2,
        grid=(N_TOK // TT,),
        in_specs=[pl.BlockSpec((TT, HALF_D), lambda i, pos, ltr: (i, 0))],
        out_specs=pl.BlockSpec(memory_space=pl.ANY),
        scratch_shapes=[pltpu.VMEM((TE, HALF_D), u32), pltpu.SemaphoreType.DMA(()), pltpu.SemaphoreType.DMA(())],
    )
    return pl.pallas_call(
        _scatter_kernel,
        grid_spec=grid_spec,
        out_shape=jax.ShapeDtypeStruct((R_MAX, HALF_D), u32),
        compiler_params=_params(1),
        name="dispatch",
    )(pos, last_tile_row, hp)


RING_BYTES = 8 * 1024 * 1024
CHUNK_ROWS = 128


def _grouped_kernel(tb_ref, nu_ref, seq_ref, nseq_ref, tseq_ref, tnt_ref, x_ref, w_hbm, o_ref, ring, wb, sem, nxt,
                    *, layer, k_dim, swiglu):
    del tb_ref
    t = pl.program_id(0)
    n_slots = ring.shape[0]
    nc = k_dim // CHUNK_ROWS
    nc_shift = nc.bit_length() - 1
    total = nc * nseq_ref[0]

    def chunk_copy(g):
        e = seq_ref[lax.shift_right_logical(g, nc_shift)]
        row = pl.multiple_of((g & (nc - 1)) * CHUNK_ROWS, CHUNK_ROWS)
        slot = g & (n_slots - 1)
        return pltpu.make_async_copy(w_hbm.at[layer, e, pl.ds(row, CHUNK_ROWS)], ring.at[slot], sem.at[slot])

    def cast_chunk(g):
        chunk_copy(g).wait()
        buf = lax.shift_right_logical(g, nc_shift) & 1
        row = pl.multiple_of((g & (nc - 1)) * CHUNK_ROWS, CHUNK_ROWS)
        wb[buf, pl.ds(row, CHUNK_ROWS), :] = ring[g & (n_slots - 1)].astype(bf16)

        @pl.when(g + n_slots < total)
        def _():
            chunk_copy(g + n_slots).start()

    def cast_next(n):
        def body(_, carry):
            cast_chunk(nxt[0])
            nxt[0] = nxt[0] + 1
            return carry

        lax.fori_loop(0, n, body, 0)

    @pl.when(t == 0)
    def _():
        nxt[0] = 0
        for g in range(n_slots):
            chunk_copy(g).start()
        cast_next(nc)

    @pl.when(t < nu_ref[0])
    def _():
        s = tseq_ref[t]
        nt = tnt_ref[t]
        share = lax.div(nc + nt - 1, nt)
        limit = jnp.minimum((s + 2) * nc, total)

        def cast_share(n):
            cast_next(jnp.maximum(jnp.minimum(n, limit - nxt[0]), 0))

        first_half = lax.shift_right_logical(share, 1)
        cast_share(first_half)
        w = wb.at[s & 1]
        if swiglu:
            xw = x_ref[...]
            lo = lax.bitcast_convert_type(xw << 16, f32).astype(bf16)
            hi = lax.bitcast_convert_type(xw & jnp.uint32(0xFFFF0000), f32).astype(bf16)
            acc = (jnp.dot(lo, w[0:HALF_D, :], preferred_element_type=f32)
                   + jnp.dot(hi, w[HALF_D:D_MODEL, :], preferred_element_type=f32))
            o_ref[...] = (jax.nn.silu(acc[:, 0:D_EXPERT]) * acc[:, D_EXPERT:2 * D_EXPERT]).astype(o_ref.dtype)
        else:
            o_ref[...] = jnp.dot(x_ref[...], w[...], preferred_element_type=f32).astype(o_ref.dtype)
        cast_share(share - first_half)


def _grouped_matmul(x, w, layer, meta, *, swiglu, out_cols, out_dtype, name):
    k_dim, n_dim = w.shape[2], w.shape[3]
    n_slots = RING_BYTES // (CHUNK_ROWS * n_dim * 4)
    grid_spec = pltpu.PrefetchScalarGridSpec(
        num_scalar_prefetch=6,
        grid=(N_ETILES,),
        in_specs=[
            pl.BlockSpec((TE, x.shape[1]), lambda t, tb, *_: (tb[t], 0)),
            pl.BlockSpec(memory_space=pl.ANY),
        ],
        out_specs=pl.BlockSpec((TE, out_cols), lambda t, tb, *_: (tb[t], 0)),
        scratch_shapes=[
            pltpu.VMEM((n_slots, CHUNK_ROWS, n_dim), f32),
            pltpu.VMEM((2, k_dim, n_dim), bf16),
            pltpu.SemaphoreType.DMA((n_slots,)),
            pltpu.SMEM((1,), i32),
        ],
    )
    return pl.pallas_call(
        functools.partial(_grouped_kernel, layer=layer, k_dim=k_dim, swiglu=swiglu),
        grid_spec=grid_spec,
        out_shape=jax.ShapeDtypeStruct((R_MAX, out_cols), out_dtype),
        compiler_params=_params(1),
        name=name,
    )(meta["tile_blk"], meta["n_used"], meta["seq_e"], meta["n_seq"], meta["tile_seq"], meta["tile_nt"], x, w)


def _gate_up(xs, w_gate_up, layer, meta):
    return _grouped_matmul(xs, w_gate_up, layer, meta, swiglu=True, out_cols=D_EXPERT, out_dtype=bf16,
                           name="moe_gate_up")


def _down(h1, w_down, layer, meta):
    return _grouped_matmul(h1, w_down, layer, meta, swiglu=False, out_cols=D_MODEL, out_dtype=f32, name="moe_down")


def _combine_kernel(pos_ref, x_ref, gate_ref, r_ref, ys_hbm, *rest, tok0, n_steps, next_norm):
    if next_norm:
        ng_ref, nsh_ref, nsc_ref, o_ref, h_ref, buf, sem = rest
    else:
        o_ref, buf, sem = rest
    i = pl.program_id(0)

    def issue(step, slot):
        base = (step * TC + tok0) * TOP_K

        def body(r, carry):
            for k in range(TOP_K):
                p = pos_ref[base + TOP_K * r + k]
                pltpu.make_async_copy(ys_hbm.at[pl.ds(p, 1)], buf.at[slot, k, pl.ds(r, 1)], sem.at[slot]).start()
            return carry

        lax.fori_loop(0, TC, body, 0, unroll=ISSUE_UNROLL)

    @pl.when(i == 0)
    def _():
        issue(0, 0)

    @pl.when(i + 1 < n_steps)
    def _():
        issue(i + 1, (i + 1) & 1)

    slot = i & 1
    for k in range(TOP_K):
        pltpu.make_async_copy(ys_hbm.at[pl.ds(0, TC)], buf.at[slot, k], sem.at[slot]).wait()
    r = r_ref[...]
    f = r[:, 2:3] * buf[slot, 0] + r[:, 3:4] * buf[slot, 1]
    y = x_ref[...] + gate_ref[...] * f
    o_ref[...] = y
    if next_norm:
        h_ref[...] = _adaln(y, ng_ref[...], nsh_ref[...], nsc_ref[...]).astype(bf16)


def _combine(x, mod, layer, route, ys, pos, *, tok0, n_tok, next_gain=None):
    blk0 = tok0 // TC
    mod_spec = lambda lyr, k: pl.BlockSpec((None, None, None, 1, D_MODEL),
                                           lambda i, pos: (lyr, _mod_row(i + blk0, TC), k, 0, 0))
    in_specs = [
        pl.BlockSpec((TC, D_MODEL), lambda i, pos: (i + blk0, 0)),
        mod_spec(layer, 5),
        pl.BlockSpec((TC, ROUTER_LANES), lambda i, pos: (i + blk0, 0)),
        pl.BlockSpec(memory_space=pl.ANY),
    ]
    args = [pos, x, mod, route, ys]
    out_specs = [pl.BlockSpec((TC, D_MODEL), lambda i, pos: (i, 0))]
    out_shape = [jax.ShapeDtypeStruct((n_tok, D_MODEL), f32)]
    if next_gain is not None:
        in_specs += [pl.BlockSpec((1, D_MODEL), lambda i, pos: (0, 0)), mod_spec(layer + 1, 0), mod_spec(layer + 1, 1)]
        args += [next_gain, mod, mod]
        out_specs.append(pl.BlockSpec((TC, D_MODEL), lambda i, pos: (i, 0)))
        out_shape.append(jax.ShapeDtypeStruct((n_tok, D_MODEL), bf16))
    grid_spec = pltpu.PrefetchScalarGridSpec(
        num_scalar_prefetch=1,
        grid=(n_tok // TC,),
        in_specs=in_specs,
        out_specs=out_specs,
        scratch_shapes=[pltpu.VMEM((2, TOP_K, TC, D_MODEL), f32), pltpu.SemaphoreType.DMA((2,))],
    )
    out = pl.pallas_call(
        functools.partial(_combine_kernel, tok0=tok0, n_steps=n_tok // TC, next_norm=next_gain is not None),
        grid_spec=grid_spec,
        out_shape=out_shape,
        compiler_params=_params(1),
        name="combine",
    )(*args)
    return out if next_gain is not None else out[0]


def _hier_moe(x, mod, layer, gain, w_group, b_group, w_expert, b_expert, w_gate_up, w_down, *, next_gain=None):
    hp, route = _router(x, mod, layer, gain, w_group, b_group, w_expert, b_expert)
    meta = _moe_layout(route)
    pos = meta["pos"]
    xs = _dispatch(hp, pos, meta["last_tile_row"])
    h1 = _gate_up(xs, w_gate_up, layer, meta)
    ys = _down(h1, w_down, layer, meta)
    if next_gain is None:
        return (_combine(x, mod, layer, route, ys, pos, tok0=0, n_tok=N_CTX),
                _combine(x, mod, layer, route, ys, pos, tok0=N_CTX, n_tok=N_LAT))
    return _combine(x, mod, layer, route, ys, pos, tok0=0, n_tok=N_TOK, next_gain=next_gain)


def kernel(x_prompt, x_sample, cache_k, cache_v, c, c_ctx, norm_gain, w_mod, b_mod, attn_w_qkv, attn_q_gain, attn_k_gain, attn_w_o, cm_w_in, cm_v_gain, cm_w_s, cm_b_s, cm_w_out, moe_w_group, moe_b_group, moe_w_expert, moe_b_expert, moe_w_gate_up, moe_w_down):
    x_ctx = x_prompt.reshape(N_CTX, D_MODEL)
    x_lat = x_sample.reshape(N_LAT, D_MODEL)
    cond =jnp.concatenate([c_ctx[None, :], c, jnp.zeros((MOD_ROWS - 1 - DEC_BATCH, D_MODEL), f32)], axis=0)
    mod = _modulation(cond, w_mod, b_mod)

    w_qkv = attn_w_qkv[0].astype(bf16)
    q_gain = attn_q_gain[0][None, :]
    k_gain = attn_k_gain[0][None, :]
    gain0 = norm_gain[0, 0][None, :]
    qkv_ctx, kv_new = _qkv_proj(x_ctx, mod, 0, gain0, w_qkv, q_gain, k_gain, tok0=0, with_cache=True)
    (qkv_lat,) = _qkv_proj(x_lat, mod, 0, gain0, w_qkv, q_gain, k_gain, tok0=N_CTX, rope_tabs=_rope_tables())
    o_ctx = _attention(qkv_ctx, n_batch=BATCH, seq=SEQ, tq=SEQ)
    ck = cache_k[:, 0].reshape(DEC_BATCH, PAST_LEN, KV_DIM)
    cv = cache_v[:, 0].reshape(DEC_BATCH, PAST_LEN, KV_DIM)
    o_lat = _attention(qkv_lat, n_batch=DEC_BATCH, seq=DEC_SEQ, tq=TQ, cache=(ck, cv))
    w_o = attn_w_o[0].astype(bf16)
    x = _proj_res(o_ctx, w_o, x_ctx, mod, 0, tok0=0)
    x = _proj_res(o_lat, w_o, x_lat, mod, 0, tok0=N_CTX, out=x)
    x, h = _hier_moe(x, mod, 0, norm_gain[0, 1][None, :], moe_w_group[0], moe_b_group[0], moe_w_expert[0],
                     moe_b_expert[0], moe_w_gate_up, moe_w_down, next_gain=norm_gain[1, 0][None, :])

    uv, ssq = _chunk_in(h, cm_w_in[0].astype(bf16))
    x = _chunk_out(uv, ssq, cm_v_gain[0][None, :], cm_w_s[0], cm_b_s[0][:, :, None], cm_w_out[0].astype(bf16),
                   x, mod, 1)
    y_ctx, y_lat = _hier_moe(x, mod, 1, norm_gain[1, 1][None, :], moe_w_group[1], moe_b_group[1], moe_w_expert[1],
                             moe_b_expert[1], moe_w_gate_up, moe_w_down)

    new_k = kv_new[:, :KV_DIM].reshape(BATCH, 1, SEQ, N_KV_HEADS, HEAD_DIM)
    new_v = kv_new[:, KV_DIM:].reshape(BATCH, 1, SEQ, N_KV_HEADS, HEAD_DIM)
    return (y_ctx.reshape(BATCH, SEQ, D_MODEL), y_lat.reshape(DEC_BATCH, DEC_SEQ, D_MODEL), new_k, new_v)
```

```python
import functools

import jax
import jax.numpy as jnp
from jax import lax
from jax.experimental import pallas as pl
from jax.experimental.pallas import tpu as pltpu

f32 = jnp.float32
bf16 = jnp.bfloat16
i32 = jnp.int32
u32 = jnp.uint32

D_MODEL = 4096
BATCH, SEQ = 16, 256
DEC_BATCH, DEC_SEQ = 4, 2048
PAST_LEN = 256
DEPTH = 2
GRID_W = 64
N_HEADS, N_KV_HEADS, HEAD_DIM = 32, 8, 128
GQA = N_HEADS // N_KV_HEADS
Q_DIM = N_HEADS * HEAD_DIM
KV_DIM = N_KV_HEADS * HEAD_DIM
QKV_DIM = Q_DIM + 2 * KV_DIM
AXIS_DIM = HEAD_DIM // 2
ROPE_THETA = 10000.0
CHUNK = 128
D_INNER = D_MODEL
N_SGU_GROUPS = 8
SGU_GROUP_DIM = D_INNER // N_SGU_GROUPS
N_EXPERT_GROUPS, EXPERTS_PER_GROUP = 4, 8
N_EXPERTS = N_EXPERT_GROUPS * EXPERTS_PER_GROUP
TOP_K = 2
D_EXPERT = D_MODEL // 4
N_MOD = 6
RMS_EPS = 1e-6

N_CTX = BATCH * SEQ
N_LAT = DEC_BATCH * DEC_SEQ
N_TOK = N_CTX + N_LAT
N_COPIES = N_TOK * TOP_K
MOD_ROWS = 8

LANES = 128
HALF_D = D_MODEL // 2

TM = 512
TM_PROJ = 1024
TN_QKV = 1024
TN_PROJ = 1024
TN_MOD = 512
TQ = 256
KEY_CHUNK = 128
TE = 256
N_ETILES = N_COPIES // TE + N_EXPERTS
R_MAX = N_ETILES * TE
CTX_KV_HEADS_PER_STEP = 4
TT = 1024
TC = 256
ISSUE_UNROLL = 8
LOG2E = 1.4426950408889634
ROUTER_LANES = 128
EXPERT_LANE0 = 32

VMEM_LIMIT = 60 * 1024 * 1024


def _params(n_axes):
    return pltpu.CompilerParams(dimension_semantics=("arbitrary",) * n_axes, vmem_limit_bytes=VMEM_LIMIT)


def _mod_row(i, tm):
    start = i * tm
    return jnp.where(start < N_CTX, 0, 1 + (start - N_CTX) // DEC_SEQ)


def _mod_spec(layer, k, tm, tn=D_MODEL, col=None, row_fn=None):
    row_fn = row_fn or (lambda i: _mod_row(i, tm))
    if col is None:
        return pl.BlockSpec((None, None, None, 1, tn), lambda i, j: (layer, row_fn(i), k, 0, 0))
    return pl.BlockSpec((None, None, None, 1, tn), lambda i, j: (layer, row_fn(i), k, 0, j))


def _adaln(x, gain, shift, scale):
    ms = jnp.mean(x * x, axis=-1, keepdims=True)
    return x * lax.rsqrt(ms + RMS_EPS) * gain * (1.0 + scale) + shift


def _mod_kernel(c_ref, w_ref, b_ref, o_ref):
    s = jax.nn.silu(c_ref[...]).astype(bf16)
    o_ref[...] = jnp.dot(s, w_ref[...].astype(bf16), preferred_element_type=f32) + b_ref[...]


def _modulation(cond, w_mod, b_mod):
    n = N_MOD * D_MODEL
    out = pl.pallas_call(
        _mod_kernel,
        grid=(DEPTH, n // TN_MOD),
        in_specs=[
            pl.BlockSpec((MOD_ROWS, D_MODEL), lambda l, j: (0, 0)),
            pl.BlockSpec((None, D_MODEL, TN_MOD), lambda l, j: (l, 0, j)),
            pl.BlockSpec((None, 1, TN_MOD), lambda l, j: (l, 0, j)),
        ],
        out_specs=pl.BlockSpec((None, MOD_ROWS, TN_MOD), lambda l, j: (l, 0, j)),
        out_shape=jax.ShapeDtypeStruct((DEPTH, MOD_ROWS, n), f32),
        compiler_params=_params(2),
        name="modulation",
    )(cond, w_mod, b_mod.reshape(DEPTH, 1, n))
    return out.reshape(DEPTH, MOD_ROWS, N_MOD, 1, D_MODEL)


def _qkv_kernel(*refs, rope, with_cache):
    x_ref, g_ref, sh_ref, sc_ref, w_ref, hg_ref = refs[:6]
    pos = 6
    if rope:
        cos_ref, sa_ref, sb_ref = refs[pos:pos + 3]
        pos += 3
    o_ref = refs[pos]
    pos += 1
    if with_cache:
        kv_ref = refs[pos]
        pos += 1
    h_scr = refs[pos]

    j = pl.program_id(1)
    nq = Q_DIM // TN_QKV
    nk = KV_DIM // TN_QKV

    @pl.when(j == 0)
    def _():
        h_scr[...] = _adaln(x_ref[...], g_ref[...], sh_ref[...], sc_ref[...]).astype(bf16)

    acc = jnp.dot(h_scr[...], w_ref[...], preferred_element_type=f32)

    is_v = j >= nq + nk
    gain = hg_ref[...]
    for hh in range(TN_QKV // HEAD_DIM):
        cols = slice(hh * HEAD_DIM, (hh + 1) * HEAD_DIM)
        a = acc[:, cols]
        y = a * lax.rsqrt(jnp.mean(a * a, axis=-1, keepdims=True) + RMS_EPS) * gain
        if rope:
            y = (y * cos_ref[...] + pltpu.roll(y, HEAD_DIM - AXIS_DIM // 2, 1) * sa_ref[...]
                 + pltpu.roll(y, AXIS_DIM // 2, 1) * sb_ref[...])
        y = jnp.where(is_v, a, y)
        o_ref[:, cols] = y.astype(bf16)
        if with_cache:
            kv_ref[:, cols] = y


def _qkv_proj(x, mod, layer, gain, w_qkv, q_gain, k_gain, *, tok0, rope_tabs=None, with_cache=False):
    n_tok = x.shape[0]
    blk0 = tok0 // TM
    nj = QKV_DIM // TN_QKV
    nq = Q_DIM // TN_QKV
    nk = KV_DIM // TN_QKV
    head_gain = jnp.stack([q_gain * (HEAD_DIM ** -0.5 * LOG2E)] * nq + [k_gain] * nk
                          + [jnp.ones_like(k_gain)] * (nj - nq - nk))
    row_fn = lambda i: _mod_row(i + blk0, TM)
    in_specs = [
        pl.BlockSpec((TM, D_MODEL), lambda i, j: (i, 0)),
        pl.BlockSpec((1, D_MODEL), lambda i, j: (0, 0)),
        _mod_spec(layer, 0, TM, row_fn=row_fn),
        _mod_spec(layer, 1, TM, row_fn=row_fn),
        pl.BlockSpec((D_MODEL, TN_QKV), lambda i, j: (0, j)),
        pl.BlockSpec((None, 1, HEAD_DIM), lambda i, j: (j, 0, 0)),
    ]
    args = [x, gain, mod, mod, w_qkv, head_gain]
    if rope_tabs is not None:
        nb = DEC_SEQ // TM
        in_specs += [pl.BlockSpec((TM, HEAD_DIM), lambda i, j: (i % nb, 0))] * 3
        args += list(rope_tabs)
    out_specs = [pl.BlockSpec((TM, TN_QKV), lambda i, j: (i, j))]
    out_shape = [jax.ShapeDtypeStruct((n_tok, QKV_DIM), bf16)]
    if with_cache:
        out_specs.append(pl.BlockSpec((TM, TN_QKV), lambda i, j: (i, jnp.maximum(j - nq, 0))))
        out_shape.append(jax.ShapeDtypeStruct((n_tok, 2 * KV_DIM), f32))
    return pl.pallas_call(
        functools.partial(_qkv_kernel, rope=rope_tabs is not None, with_cache=with_cache),
        grid=(n_tok // TM, nj),
        in_specs=in_specs,
        out_specs=out_specs,
        out_shape=out_shape,
        scratch_shapes=[pltpu.VMEM((TM, D_MODEL), bf16)],
        compiler_params=_params(2),
        name="qkv_rope" if rope_tabs is not None else "qkv_ctx",
    )(*args)


def _rope_tables():
    t = jnp.arange(DEC_SEQ)
    row = (t // GRID_W).astype(f32)
    col = (t % GRID_W).astype(f32)
    inv_freq = ROPE_THETA ** (-jnp.arange(0, AXIS_DIM, 2, dtype=f32) / AXIS_DIM)
    lane = jnp.arange(HEAD_DIM)
    freq = inv_freq[lane % (AXIS_DIM // 2)]
    pos = jnp.where(lane[None, :] < AXIS_DIM, row[:, None], col[:, None])
    ang = pos * freq[None, :]
    first = (lane % AXIS_DIM) < (AXIS_DIM // 2)
    sin = jnp.sin(ang)
    return jnp.cos(ang), jnp.where(first[None, :], -sin, 0.0), jnp.where(first[None, :], 0.0, sin)


def _attn_kernel(*refs, with_cache):
    q_ref, k_ref, v_ref = refs[:3]
    if with_cache:
        ck_ref, cv_ref, o_ref, k_all, vt_all = refs[3:8]
    else:
        o_ref, k_all, vt_all = refs[3:6]
    n_keys = k_all.shape[0]
    kv_heads = k_ref.shape[1] // HEAD_DIM
    kc = KEY_CHUNK

    @pl.when(pl.program_id(2) == 0)
    def _():
        off = 0
        if with_cache:
            k_all[0:PAST_LEN, :] = ck_ref[...].astype(bf16)
            for c in range(PAST_LEN // kc):
                vt_all[:, c * kc:(c + 1) * kc] = cv_ref[c * kc:(c + 1) * kc, :].T.astype(bf16)
            off = PAST_LEN
        k_all[off:n_keys, :] = k_ref[...]
        for kv in range(kv_heads):
            cols = slice(kv * HEAD_DIM, (kv + 1) * HEAD_DIM)
            for c in range((n_keys - off) // kc):
                vt_all[cols, off + c * kc:off + (c + 1) * kc] = (
                    v_ref[c * kc:(c + 1) * kc, cols].astype(f32).T.astype(bf16))

    nt_dims = (((1,), (1,)), ((), ()))
    for head in range(kv_heads * GQA):
        g = head
        kv_cols = slice((head // GQA) * HEAD_DIM, (head // GQA + 1) * HEAD_DIM)
        qg = q_ref[:, g * HEAD_DIM:(g + 1) * HEAD_DIM]
        m = l = acc = None
        for c in range(n_keys // kc):
            st = lax.dot_general(k_all[c * kc:(c + 1) * kc, kv_cols], qg, nt_dims, preferred_element_type=f32)
            ms = jnp.max(st, axis=0, keepdims=True)
            vt = vt_all[kv_cols, c * kc:(c + 1) * kc]
            if m is None:
                m = ms
                p = jnp.exp2(st - m)
                l = jnp.sum(p, axis=0, keepdims=True)
                acc = jnp.dot(vt, p.astype(bf16), preferred_element_type=f32)
            else:
                m_new = jnp.maximum(m, ms)
                alpha = jnp.exp2(m - m_new)
                p = jnp.exp2(st - m_new)
                l = alpha * l + jnp.sum(p, axis=0, keepdims=True)
                acc = alpha * acc + jnp.dot(vt, p.astype(bf16), preferred_element_type=f32)
                m = m_new
        o_ref[:, g * HEAD_DIM:(g + 1) * HEAD_DIM] = (acc / l).T.astype(bf16)


def _attention(qkv, *, n_batch, seq, tq, cache=None):
    nqb = seq // tq
    n_keys = seq + (PAST_LEN if cache is not None else 0)
    kvb = 1 if cache is not None else CTX_KV_HEADS_PER_STEP
    kv_w = kvb * HEAD_DIM
    kcol = Q_DIM // kv_w
    vcol = (Q_DIM + KV_DIM) // kv_w
    in_specs = [
        pl.BlockSpec((tq, GQA * kv_w), lambda b, h, i: (b * nqb + i, h)),
        pl.BlockSpec((seq, kv_w), lambda b, h, i: (b, kcol + h)),
        pl.BlockSpec((seq, kv_w), lambda b, h, i: (b, vcol + h)),
    ]
    args = [qkv, qkv, qkv]
    if cache is not None:
        in_specs += [pl.BlockSpec((None, PAST_LEN, HEAD_DIM), lambda b, h, i: (b, 0, h))] * 2
        args += list(cache)
    return pl.pallas_call(
        functools.partial(_attn_kernel, with_cache=cache is not None),
        grid=(n_batch, N_KV_HEADS // kvb, nqb),
        in_specs=in_specs,
        out_specs=pl.BlockSpec((tq, GQA * kv_w), lambda b, h, i: (b * nqb + i, h)),
        out_shape=jax.ShapeDtypeStruct((n_batch * seq, Q_DIM), bf16),
        scratch_shapes=[pltpu.VMEM((n_keys, kv_w), bf16), pltpu.VMEM((kv_w, n_keys), bf16)],
        compiler_params=_params(3),
        name="attn_lat" if cache is not None else "attn_ctx",
    )(*args)


def _proj_res_kernel(a_ref, w_ref, x_ref, gate_ref, *rest):
    o_ref = rest[-1]
    o_ref[...] = x_ref[...] + gate_ref[...] * jnp.dot(a_ref[...], w_ref[...], preferred_element_type=f32)


def _proj_res(a, w, x, mod, layer, *, tok0, out=None):
    n_tok = a.shape[0]
    tm = TM_PROJ
    blk0 = tok0 // tm
    row_fn = lambda i: _mod_row(i + blk0, tm)
    in_specs = [
        pl.BlockSpec((tm, a.shape[1]), lambda i, j: (i, 0)),
        pl.BlockSpec((a.shape[1], TN_PROJ), lambda i, j: (0, j)),
        pl.BlockSpec((tm, TN_PROJ), lambda i, j: (i, j)),
        _mod_spec(layer, 2, tm, tn=TN_PROJ, col=True, row_fn=row_fn),
    ]
    args = [a, w, x, mod]
    aliases = {}
    if out is not None:
        in_specs.append(pl.BlockSpec(memory_space=pl.ANY))
        args.append(out)
        aliases = {len(args) - 1: 0}
    return pl.pallas_call(
        _proj_res_kernel,
        grid=(n_tok // tm, D_MODEL // TN_PROJ),
        in_specs=in_specs,
        out_specs=pl.BlockSpec((tm, TN_PROJ), lambda i, j: (i + blk0, j)),
        out_shape=jax.ShapeDtypeStruct((N_TOK, D_MODEL), f32),
        input_output_aliases=aliases,
        compiler_params=_params(2),
        name="proj_res",
    )(*args)


def _win_kernel(h_ref, w_ref, o_ref, ssq_ref, *, tn):
    j = pl.program_id(1)
    nu = D_INNER // tn

    @pl.when(j == 0)
    def _():
        ssq_ref[...] = jnp.zeros_like(ssq_ref)

    y = jax.nn.gelu(jnp.dot(h_ref[...], w_ref[...], preferred_element_type=f32))
    o_ref[...] = y.astype(bf16)

    y2 = y * y
    part = y2[:, 0:LANES]
    for c in range(1, tn // LANES):
        part = part + y2[:, c * LANES:(c + 1) * LANES]
    ssq_ref[...] += jnp.where(j >= nu, part, 0.0)


def _chunk_in(h, w_in):
    tn = TN_QKV
    tm = TM_PROJ
    return pl.pallas_call(
        functools.partial(_win_kernel, tn=tn),
        grid=(N_TOK // tm, 2 * D_INNER // tn),
        in_specs=[
            pl.BlockSpec((tm, D_MODEL), lambda i, j: (i, 0)),
            pl.BlockSpec((D_MODEL, tn), lambda i, j: (0, j)),
        ],
        out_specs=[
            pl.BlockSpec((tm, tn), lambda i, j: (i, j)),
            pl.BlockSpec((tm, LANES), lambda i, j: (i, 0)),
        ],
        out_shape=[jax.ShapeDtypeStruct((N_TOK, 2 * D_INNER), bf16), jax.ShapeDtypeStruct((N_TOK, LANES), f32)],
        compiler_params=_params(2),
        name="chunk_in",
    )(h, w_in)


def _sgu_out_kernel(u_ref, v_ref, ssq_ref, vg_ref, ws_ref, bs_ref, w_ref, x_ref, gate_ref, o_ref, a_scr):
    j = pl.program_id(1)

    @pl.when(j == 0)
    def _():
        rstd = lax.rsqrt(jnp.sum(ssq_ref[...], axis=-1, keepdims=True) * (1.0 / D_INNER) + RMS_EPS)
        for c in range(TM // CHUNK):
            rows = slice(c * CHUNK, (c + 1) * CHUNK)
            rs = rstd[rows]
            for g in range(N_SGU_GROUPS):
                cols = slice(g * SGU_GROUP_DIM, (g + 1) * SGU_GROUP_DIM)
                vn = (v_ref[rows, cols].astype(f32) * rs * vg_ref[:, cols]).astype(bf16)
                vm = jnp.dot(ws_ref[g].astype(bf16), vn, preferred_element_type=f32) + bs_ref[g]
                a_scr[rows, cols] = (u_ref[rows, cols].astype(f32) * vm).astype(bf16)

    o_ref[...] = x_ref[...] + gate_ref[...] * jnp.dot(a_scr[...], w_ref[...], preferred_element_type=f32)


def _chunk_out(uv, ssq, v_gain, w_s, b_s, w_out, x, mod, layer):
    return pl.pallas_call(
        _sgu_out_kernel,
        grid=(N_TOK // TM, D_MODEL // TN_PROJ),
        in_specs=[
            pl.BlockSpec((TM, D_INNER), lambda i, j: (i, 0)),
            pl.BlockSpec((TM, D_INNER), lambda i, j: (i, 1)),
            pl.BlockSpec((TM, LANES), lambda i, j: (i, 0)),
            pl.BlockSpec((1, D_INNER), lambda i, j: (0, 0)),
            pl.BlockSpec((N_SGU_GROUPS, CHUNK, CHUNK), lambda i, j: (0, 0, 0)),
            pl.BlockSpec((N_SGU_GROUPS, CHUNK, 1), lambda i, j: (0, 0, 0)),
            pl.BlockSpec((D_INNER, TN_PROJ), lambda i, j: (0, j)),
            pl.BlockSpec((TM, TN_PROJ), lambda i, j: (i, j)),
            _mod_spec(layer, 2, TM, tn=TN_PROJ, col=True),
        ],
        out_specs=pl.BlockSpec((TM, TN_PROJ), lambda i, j: (i, j)),
        out_shape=jax.ShapeDtypeStruct((N_TOK, D_MODEL), f32),
        scratch_shapes=[pltpu.VMEM((TM, D_INNER), bf16)],
        compiler_params=_params(2),
        name="sgu_out",
    )(uv, uv, ssq, v_gain, w_s, b_s, w_out, x, mod)


def _router_kernel(x_ref, g_ref, sh_ref, sc_ref, whi_ref, wlo_ref, b_ref, hp_ref, r_ref):
    h = _adaln(x_ref[...], g_ref[...], sh_ref[...], sc_ref[...])
    hb = h.astype(bf16)
    hf = hb.astype(f32)
    lo = (h - hf).astype(bf16)
    logits = (jnp.dot(hb, whi_ref[...], preferred_element_type=f32)
              + (jnp.dot(lo, whi_ref[...], preferred_element_type=f32)
                 + jnp.dot(hb, wlo_ref[...], preferred_element_type=f32))) + b_ref[...]

    bits = lax.bitcast_convert_type(hf, u32)
    hp_ref[...] = (bits[:, :HALF_D] >> 16) | bits[:, HALF_D:]

    lane = lax.broadcasted_iota(i32, logits.shape, 1).astype(f32)
    neg = -jnp.inf
    big = float(ROUTER_LANES)
    lg = jnp.where(lane < N_EXPERT_GROUPS, logits, neg)
    mg = jnp.max(lg, axis=-1, keepdims=True)
    pg_top = 1.0 / jnp.sum(jnp.exp(lg - mg), axis=-1, keepdims=True)
    grp = jnp.min(jnp.where(lg == mg, lane, big), axis=-1, keepdims=True)
    lane0 = EXPERT_LANE0 + EXPERTS_PER_GROUP * grp
    le = jnp.where((lane >= lane0) & (lane < lane0 + EXPERTS_PER_GROUP), logits, neg)
    m1 = jnp.max(le, axis=-1, keepdims=True)
    i1 = jnp.min(jnp.where(le == m1, lane, big), axis=-1, keepdims=True)
    le2 = jnp.where(lane == i1, neg, le)
    m2 = jnp.max(le2, axis=-1, keepdims=True)
    i2 = jnp.min(jnp.where(le2 == m2, lane, big), axis=-1, keepdims=True)
    e2 = jnp.exp(m2 - m1)
    p1 = 1.0 / (1.0 + e2)
    p2 = e2 * p1
    r_ref[...] = jnp.where(lane == 0, i1 - EXPERT_LANE0,
                           jnp.where(lane == 1, i2 - EXPERT_LANE0,
                                     jnp.where(lane == 2, pg_top * p1,
                                               jnp.where(lane == 3, pg_top * p2, 0.0))))


def _router(x, mod, layer, gain, w_group, b_group, w_expert, b_expert):
    tm = 256
    pad0 = jnp.zeros((D_MODEL, EXPERT_LANE0 - N_EXPERT_GROUPS), f32)
    pad1 = jnp.zeros((D_MODEL, ROUTER_LANES - EXPERT_LANE0 - N_EXPERTS), f32)
    w_r = jnp.concatenate([w_group, pad0, w_expert, pad1], axis=1)
    w_hi = w_r.astype(bf16)
    w_lo = (w_r - w_hi.astype(f32)).astype(bf16)
    b_r = jnp.concatenate([b_group, pad0[0], b_expert, pad1[0]])[None, :]
    return pl.pallas_call(
        _router_kernel,
        grid=(N_TOK // tm, 1),
        in_specs=[
            pl.BlockSpec((tm, D_MODEL), lambda i, j: (i, 0)),
            pl.BlockSpec((1, D_MODEL), lambda i, j: (0, 0)),
            _mod_spec(layer, 3, tm),
            _mod_spec(layer, 4, tm),
            pl.BlockSpec((D_MODEL, ROUTER_LANES), lambda i, j: (0, 0)),
            pl.BlockSpec((D_MODEL, ROUTER_LANES), lambda i, j: (0, 0)),
            pl.BlockSpec((1, ROUTER_LANES), lambda i, j: (0, 0)),
        ],
        out_specs=[
            pl.BlockSpec((tm, HALF_D), lambda i, j: (i, 0)),
            pl.BlockSpec((tm, ROUTER_LANES), lambda i, j: (i, 0)),
        ],
        out_shape=[jax.ShapeDtypeStruct((N_TOK, HALF_D), u32), jax.ShapeDtypeStruct((N_TOK, ROUTER_LANES), f32)],
        compiler_params=_params(2),
        name="router",
    )(x, gain, mod, mod, w_hi, w_lo, b_r)


def _moe_layout(route):
    flat_e = route[:, 0:TOP_K].astype(i32).reshape(-1)
    onehot = (flat_e[:, None] == jnp.arange(N_EXPERTS, dtype=i32)[None, :]).astype(i32)
    counts = jnp.sum(onehot, axis=0)
    pcounts = ((counts + TE - 1) // TE) * TE
    pend = jnp.cumsum(pcounts)
    pstart = pend - pcounts
    n_used = pend[-1] // TE
    tiles = jnp.arange(N_ETILES, dtype=i32)
    tile_e = jnp.sum((tiles[:, None] * TE >= pend[None, :]).astype(i32), axis=1)
    last_e = jnp.max(jnp.where(tiles < n_used, tile_e, 0))
    tile_e = jnp.minimum(tile_e, last_e)
    tile_blk = jnp.minimum(tiles, n_used - 1)
    last_tile_row = jnp.where(counts > 0, pend - TE, -1)
    rank = jnp.sum(onehot * (jnp.cumsum(onehot, axis=0) - 1), axis=1)
    pos = jnp.sum(onehot * pstart[None, :], axis=1) + rank
    experts = jnp.arange(N_EXPERTS, dtype=i32)
    nonempty = (counts > 0).astype(i32)
    seq_of_e = jnp.cumsum(nonempty) - 1
    n_seq = jnp.sum(nonempty)
    seq_e = jnp.sum(jnp.where((seq_of_e[None, :] == experts[:, None]) & (nonempty[None, :] > 0), experts[None, :], 0),
                    axis=1)
    tile_onehot = (tile_e[:, None] == experts[None, :]).astype(i32)
    tile_seq = jnp.sum(tile_onehot * seq_of_e[None, :], axis=1)
    tile_nt = jnp.maximum(jnp.sum(tile_onehot * (pcounts // TE)[None, :], axis=1), 1)
    as_i32 = lambda a: a.astype(i32)
    return dict(tile_e=as_i32(tile_e), tile_blk=as_i32(tile_blk), n_used=as_i32(n_used.reshape(1)),
                last_tile_row=as_i32(last_tile_row), pos=as_i32(pos), seq_e=as_i32(seq_e),
                n_seq=as_i32(n_seq.reshape(1)), tile_seq=as_i32(tile_seq), tile_nt=as_i32(tile_nt))


def _scatter_kernel(pos_ref, ltr_ref, hp_ref, xs_hbm, zero_scr, sem, zsem):
    i = pl.program_id(0)

    @pl.when(i == 0)
    def _():
        zero_scr[...] = jnp.zeros_like(zero_scr)

        def zero_copy(e):
            row0 = pl.multiple_of(ltr_ref[e], TE)
            return pltpu.make_async_copy(zero_scr, xs_hbm.at[pl.ds(row0, TE)], zsem)

        for e in range(N_EXPERTS):
            @pl.when(ltr_ref[e] >= 0)
            def _():
                zero_copy(e).start()
        for e in range(N_EXPERTS):
            @pl.when(ltr_ref[e] >= 0)
            def _():
                zero_copy(e).wait()

    base = i * (TT * TOP_K)

    def issue(r, carry):
        for k in range(TOP_K):
            p = pos_ref[base + TOP_K * r + k]
            pltpu.make_async_copy(hp_ref.at[pl.ds(r, 1)], xs_hbm.at[pl.ds(p, 1)], sem).start()
        return carry

    lax.fori_loop(0, TT, issue, 0, unroll=ISSUE_UNROLL)
    for k in range(TOP_K):
        pltpu.make_async_copy(hp_ref, xs_hbm.at[pl.ds(0, TT)], sem).wait()


def _dispatch(hp, pos, last_tile_row):
    grid_spec = pltpu.PrefetchScalarGridSpec(
        num_scalar_prefetch=2,
        grid=(N_TOK // TT,),
        in_specs=[pl.BlockSpec((TT, HALF_D), lambda i, pos, ltr: (i, 0))],
        out_specs=pl.BlockSpec(memory_space=pl.ANY),
        scratch_shapes=[pltpu.VMEM((TE, HALF_D), u32), pltpu.SemaphoreType.DMA(()), pltpu.SemaphoreType.DMA(())],
    )
    return pl.pallas_call(
        _scatter_kernel,
        grid_spec=grid_spec,
        out_shape=jax.ShapeDtypeStruct((R_MAX, HALF_D), u32),
        compiler_params=_params(1),
        name="dispatch",
    )(pos, last_tile_row, hp)


RING_BYTES = 16 * 1024 * 1024
CHUNK_ROWS = 128


def _grouped_kernel(tb_ref, nu_ref, seq_ref, nseq_ref, tseq_ref, tnt_ref, x_ref, w_hbm, o_ref, ring, wb, sem, nxt,
                    *, layer, k_dim, swiglu):
    del tb_ref
    t = pl.program_id(0)
    n_slots = ring.shape[0]
    nc = k_dim // CHUNK_ROWS
    nc_shift = nc.bit_length() - 1
    total = nc * nseq_ref[0]

    def chunk_copy(g):
        e = seq_ref[lax.shift_right_logical(g, nc_shift)]
        row = pl.multiple_of((g & (nc - 1)) * CHUNK_ROWS, CHUNK_ROWS)
        slot = g & (n_slots - 1)
        return pltpu.make_async_copy(w_hbm.at[layer, e, pl.ds(row, CHUNK_ROWS)], ring.at[slot], sem.at[slot])

    def cast_chunk(g):
        chunk_copy(g).wait()
        buf = lax.shift_right_logical(g, nc_shift) & 1
        row = pl.multiple_of((g & (nc - 1)) * CHUNK_ROWS, CHUNK_ROWS)
        wb[buf, pl.ds(row, CHUNK_ROWS), :] = ring[g & (n_slots - 1)].astype(bf16)

        @pl.when(g + n_slots < total)
        def _():
            chunk_copy(g + n_slots).start()

    def cast_next(n):
        def body(_, carry):
            cast_chunk(nxt[0])
            nxt[0] = nxt[0] + 1
            return carry

        lax.fori_loop(0, n, body, 0)

    @pl.when(t == 0)
    def _():
        nxt[0] = 0
        for g in range(n_slots):
            chunk_copy(g).start()
        cast_next(nc)

    @pl.when(t < nu_ref[0])
    def _():
        s = tseq_ref[t]
        nt = tnt_ref[t]
        share = lax.div(nc + nt - 1, nt)
        limit = jnp.minimum((s + 2) * nc, total)

        def cast_share(n):
            cast_next(jnp.maximum(jnp.minimum(n, limit - nxt[0]), 0))

        first_half = lax.shift_right_logical(share, 1)
        cast_share(first_half)
        w = wb.at[s & 1]
        if swiglu:
            xw = x_ref[...]
            lo = lax.bitcast_convert_type(xw << 16, f32).astype(bf16)
            hi = lax.bitcast_convert_type(xw & jnp.uint32(0xFFFF0000), f32).astype(bf16)
            acc = (jnp.dot(lo, w[0:HALF_D, :], preferred_element_type=f32)
                   + jnp.dot(hi, w[HALF_D:D_MODEL, :], preferred_element_type=f32))
            o_ref[...] = (jax.nn.silu(acc[:, 0:D_EXPERT]) * acc[:, D_EXPERT:2 * D_EXPERT]).astype(o_ref.dtype)
        else:
            o_ref[...] = jnp.dot(x_ref[...], w[...], preferred_element_type=f32).astype(o_ref.dtype)
        cast_share(share - first_half)


def _grouped_matmul(x, w, layer, meta, *, swiglu, out_cols, out_dtype, name):
    k_dim, n_dim = w.shape[2], w.shape[3]
    n_slots = RING_BYTES // (CHUNK_ROWS * n_dim * 4)
    grid_spec = pltpu.PrefetchScalarGridSpec(
        num_scalar_prefetch=6,
        grid=(N_ETILES,),
        in_specs=[
            pl.BlockSpec((TE, x.shape[1]), lambda t, tb, *_: (tb[t], 0)),
            pl.BlockSpec(memory_space=pl.ANY),
        ],
        out_specs=pl.BlockSpec((TE, out_cols), lambda t, tb, *_: (tb[t], 0)),
        scratch_shapes=[
            pltpu.VMEM((n_slots, CHUNK_ROWS, n_dim), f32),
            pltpu.VMEM((2, k_dim, n_dim), bf16),
            pltpu.SemaphoreType.DMA((n_slots,)),
            pltpu.SMEM((1,), i32),
        ],
    )
    return pl.pallas_call(
        functools.partial(_grouped_kernel, layer=layer, k_dim=k_dim, swiglu=swiglu),
        grid_spec=grid_spec,
        out_shape=jax.ShapeDtypeStruct((R_MAX, out_cols), out_dtype),
        compiler_params=_params(1),
        name=name,
    )(meta["tile_blk"], meta["n_used"], meta["seq_e"], meta["n_seq"], meta["tile_seq"], meta["tile_nt"], x, w)


def _gate_up(xs, w_gate_up, layer, meta):
    return _grouped_matmul(xs, w_gate_up, layer, meta, swiglu=True, out_cols=D_EXPERT, out_dtype=bf16,
                           name="moe_gate_up")


def _down(h1, w_down, layer, meta):
    return _grouped_matmul(h1, w_down, layer, meta, swiglu=False, out_cols=D_MODEL, out_dtype=f32, name="moe_down")


def _combine_kernel(pos_ref, x_ref, gate_ref, r_ref, ys_hbm, *rest, tok0, n_steps, next_norm):
    if next_norm:
        ng_ref, nsh_ref, nsc_ref, o_ref, h_ref, buf, sem = rest
    else:
        o_ref, buf, sem = rest
    i = pl.program_id(0)

    def issue(step, slot):
        base = (step * TC + tok0) * TOP_K

        def body(r, carry):
            for k in range(TOP_K):
                p = pos_ref[base + TOP_K * r + k]
                pltpu.make_async_copy(ys_hbm.at[pl.ds(p, 1)], buf.at[slot, k, pl.ds(r, 1)], sem.at[slot]).start()
            return carry

        lax.fori_loop(0, TC, body, 0, unroll=ISSUE_UNROLL)

    @pl.when(i == 0)
    def _():
        issue(0, 0)

    @pl.when(i + 1 < n_steps)
    def _():
        issue(i + 1, (i + 1) & 1)

    slot = i & 1
    for k in range(TOP_K):
        pltpu.make_async_copy(ys_hbm.at[pl.ds(0, TC)], buf.at[slot, k], sem.at[slot]).wait()
    r = r_ref[...]
    f = r[:, 2:3] * buf[slot, 0] + r[:, 3:4] * buf[slot, 1]
    y = x_ref[...] + gate_ref[...] * f
    o_ref[...] = y
    if next_norm:
        h_ref[...] = _adaln(y, ng_ref[...], nsh_ref[...], nsc_ref[...]).astype(bf16)


def _combine(x, mod, layer, route, ys, pos, *, tok0, n_tok, next_gain=None):
    blk0 = tok0 // TC
    mod_spec = lambda lyr, k: pl.BlockSpec((None, None, None, 1, D_MODEL),
                                           lambda i, pos: (lyr, _mod_row(i + blk0, TC), k, 0, 0))
    in_specs = [
        pl.BlockSpec((TC, D_MODEL), lambda i, pos: (i + blk0, 0)),
        mod_spec(layer, 5),
        pl.BlockSpec((TC, ROUTER_LANES), lambda i, pos: (i + blk0, 0)),
        pl.BlockSpec(memory_space=pl.ANY),
    ]
    args = [pos, x, mod, route, ys]
    out_specs = [pl.BlockSpec((TC, D_MODEL), lambda i, pos: (i, 0))]
    out_shape = [jax.ShapeDtypeStruct((n_tok, D_MODEL), f32)]
    if next_gain is not None:
        in_specs += [pl.BlockSpec((1, D_MODEL), lambda i, pos: (0, 0)), mod_spec(layer + 1, 0), mod_spec(layer + 1, 1)]
        args += [next_gain, mod, mod]
        out_specs.append(pl.BlockSpec((TC, D_MODEL), lambda i, pos: (i, 0)))
        out_shape.append(jax.ShapeDtypeStruct((n_tok, D_MODEL), bf16))
    grid_spec = pltpu.PrefetchScalarGridSpec(
        num_scalar_prefetch=1,
        grid=(n_tok // TC,),
        in_specs=in_specs,
        out_specs=out_specs,
        scratch_shapes=[pltpu.VMEM((2, TOP_K, TC, D_MODEL), f32), pltpu.SemaphoreType.DMA((2,))],
    )
    out = pl.pallas_call(
        functools.partial(_combine_kernel, tok0=tok0, n_steps=n_tok // TC, next_norm=next_gain is not None),
        grid_spec=grid_spec,
        out_shape=out_shape,
        compiler_params=_params(1),
        name="combine",
    )(*args)
    return out if next_gain is not None else out[0]


def _hier_moe(x, mod, layer, gain, w_group, b_group, w_expert, b_expert, w_gate_up, w_down, *, next_gain=None):
    hp, route = _router(x, mod, layer, gain, w_group, b_group, w_expert, b_expert)
    meta = _moe_layout(route)
    pos = meta["pos"]
    xs = _dispatch(hp, pos, meta["last_tile_row"])
    h1 = _gate_up(xs, w_gate_up, layer, meta)
    ys = _down(h1, w_down, layer, meta)
    if next_gain is None:
        return (_combine(x, mod, layer, route, ys, pos, tok0=0, n_tok=N_CTX),
                _combine(x, mod, layer, route, ys, pos, tok0=N_CTX, n_tok=N_LAT))
    return _combine(x, mod, layer, route, ys, pos, tok0=0, n_tok=N_TOK, next_gain=next_gain)


def kernel(x_prompt, x_sample, cache_k, cache_v, c, c_ctx, norm_gain, w_mod, b_mod, attn_w_qkv, attn_q_gain, attn_k_gain, attn_w_o, cm_w_in, cm_v_gain, cm_w_s, cm_b_s, cm_w_out, moe_w_group, moe_b_group, moe_w_expert, moe_b_expert, moe_w_gate_up, moe_w_down):
    x_ctx = x_prompt.reshape(N_CTX, D_MODEL)
    x_lat = x_sample.reshape(N_LAT, D_MODEL)
    cond =jnp.concatenate([c_ctx[None, :], c, jnp.zeros((MOD_ROWS - 1 - DEC_BATCH, D_MODEL), f32)], axis=0)
    mod = _modulation(cond, w_mod, b_mod)

    w_qkv = attn_w_qkv[0].astype(bf16)
    q_gain = attn_q_gain[0][None, :]
    k_gain = attn_k_gain[0][None, :]
    gain0 = norm_gain[0, 0][None, :]
    qkv_ctx, kv_new = _qkv_proj(x_ctx, mod, 0, gain0, w_qkv, q_gain, k_gain, tok0=0, with_cache=True)
    (qkv_lat,) = _qkv_proj(x_lat, mod, 0, gain0, w_qkv, q_gain, k_gain, tok0=N_CTX, rope_tabs=_rope_tables())
    o_ctx = _attention(qkv_ctx, n_batch=BATCH, seq=SEQ, tq=SEQ)
    ck = cache_k[:, 0].reshape(DEC_BATCH, PAST_LEN, KV_DIM)
    cv = cache_v[:, 0].reshape(DEC_BATCH, PAST_LEN, KV_DIM)
    o_lat = _attention(qkv_lat, n_batch=DEC_BATCH, seq=DEC_SEQ, tq=TQ, cache=(ck, cv))
    w_o = attn_w_o[0].astype(bf16)
    x = _proj_res(o_ctx, w_o, x_ctx, mod, 0, tok0=0)
    x = _proj_res(o_lat, w_o, x_lat, mod, 0, tok0=N_CTX, out=x)
    x, h = _hier_moe(x, mod, 0, norm_gain[0, 1][None, :], moe_w_group[0], moe_b_group[0], moe_w_expert[0],
                     moe_b_expert[0], moe_w_gate_up, moe_w_down, next_gain=norm_gain[1, 0][None, :])

    uv, ssq = _chunk_in(h, cm_w_in[0].astype(bf16))
    x = _chunk_out(uv, ssq, cm_v_gain[0][None, :], cm_w_s[0], cm_b_s[0][:, :, None], cm_w_out[0].astype(bf16),
                   x, mod, 1)
    y_ctx, y_lat = _hier_moe(x, mod, 1, norm_gain[1, 1][None, :], moe_w_group[1], moe_b_group[1], moe_w_expert[1],
                             moe_b_expert[1], moe_w_gate_up, moe_w_down)

    new_k = kv_new[:, :KV_DIM].reshape(BATCH, 1, SEQ, N_KV_HEADS, HEAD_DIM)
    new_v = kv_new[:, KV_DIM:].reshape(BATCH, 1, SEQ, N_KV_HEADS, HEAD_DIM)
    return (y_ctx.reshape(BATCH, SEQ, D_MODEL), y_lat.reshape(DEC_BATCH, DEC_SEQ, D_MODEL), new_k, new_v)
```
